```python
import jax, jax.numpy as jnp
from jax import lax
import numpy as np

D_MODEL = 1024
BATCH = 32
SEQ = 256
DEPTH = 2
DEC_BATCH = 8
DEC_SEQ = 4096
PAST_LEN = 512

GRID_W = 64
D_A = 512
CONV_A = 31
D_B = 512
CONV_B = 3
N_HEADS = 8
HEAD_DIM = 64
D_C = N_HEADS * HEAD_DIM
WIN_ROWS = 8
WIN_COLS = 16
Q_BLOCK_COLS = 16
K_BLOCK_COLS = 32
N_COL_BLOCKS = GRID_W // Q_BLOCK_COLS
D_FF = 2816
N_MOD = 9
N_IN = 2 * D_A + 3 * D_B + 3 * D_C + 3 * D_MODEL
CTX_Q_BLOCK = 128
EPS = 1e-6
NEG_INF = -1e30

kernel_name = "hybrid_conv_natten_prefix_dit_step"


def _rms(x, g):
    xf = x.astype(jnp.float32)
    y = xf * lax.rsqrt(jnp.mean(jnp.square(xf), axis=-1, keepdims=True) + EPS)
    return (y * g.astype(jnp.float32)).astype(x.dtype)


def _layer_norm(x, g, b):
    xf = x.astype(jnp.float32)
    mu = jnp.mean(xf, axis=-1, keepdims=True)
    xc = xf - mu
    y = xc * lax.rsqrt(jnp.mean(jnp.square(xc), axis=-1, keepdims=True) + EPS)
    return (y * g.astype(jnp.float32) + b.astype(jnp.float32)).astype(x.dtype)


def _swiglu(u, w_gate, w_up, w_down):
    return (jax.nn.silu(u @ w_gate) * (u @ w_up)) @ w_down


def _modulation(cvec, w_ada, b_ada):
    m = (jax.nn.silu(cvec) @ w_ada + b_ada).reshape(cvec.shape[0], N_MOD, D_MODEL)
    return [m[:, i][:, None, :] for i in range(N_MOD)]


def _dwconv(x, w, b=None):
    k = w.shape[0]
    y = lax.conv_general_dilated(
        x, w[:, None, :].astype(x.dtype), window_strides=(1,),
        padding=[(k // 2, k // 2)], dimension_numbers=('NWC', 'WIO', 'NWC'),
        feature_group_count=x.shape[-1])
    return y if b is None else y + b


def _project(u, w_in):
    sizes = [D_A, D_A, D_B, D_B, D_B, D_C, D_C, D_C, D_MODEL, D_MODEL]
    idx = [int(i) for i in np.cumsum(sizes)]
    return jnp.split(u @ w_in, idx, axis=-1)


def _heads(t):
    return t.reshape(*t.shape[:-1], N_HEADS, HEAD_DIM)


def _context_attention(q, k, v):
    b, t = q.shape[:2]
    scale = HEAD_DIM ** -0.5
    qb = q.reshape(b, t // CTX_Q_BLOCK, CTX_Q_BLOCK, N_HEADS, HEAD_DIM).transpose(1, 0, 2, 3, 4)

    def block(qi):
        s = jnp.einsum('bqhd,bkhd->bhqk', qi, k).astype(jnp.float32) * scale
        p = jax.nn.softmax(s, axis=-1).astype(v.dtype)
        return jnp.einsum('bhqk,bkhd->bqhd', p, v)

    o = lax.map(block, qb)
    return o.transpose(1, 0, 2, 3, 4).reshape(b, t, D_C)


def _neighbourhood_attention(q, k, v, k_ctx, v_ctx, rpb):
    b, t = q.shape[:2]
    rows = t // GRID_W
    kr = min(WIN_ROWS, rows)
    n_loc = kr * K_BLOCK_COLS
    scale = HEAD_DIM ** -0.5
    qg = q.reshape(b, rows, N_COL_BLOCKS, Q_BLOCK_COLS, N_HEADS, HEAD_DIM)
    kg = k.reshape(b, rows, GRID_W, N_HEADS, HEAD_DIM)
    vg = v.reshape(b, rows, GRID_W, N_HEADS, HEAD_DIM)
    q_cols = np.arange(GRID_W).reshape(N_COL_BLOCKS, Q_BLOCK_COLS)
    win_start = np.clip(q_cols - WIN_COLS // 2, 0, GRID_W - WIN_COLS)
    blk_start = np.clip(np.arange(N_COL_BLOCKS) * Q_BLOCK_COLS - WIN_COLS // 2, 0, GRID_W - K_BLOCK_COLS)
    k_cols = blk_start[:, None] + np.arange(K_BLOCK_COLS)
    kc = k_cols[:, None, :]
    col_valid = (kc >= win_start[..., None]) & (kc < win_start[..., None] + WIN_COLS)
    mask = jnp.asarray(np.broadcast_to(col_valid[:, :, None, :],
                                       (N_COL_BLOCKS, Q_BLOCK_COLS, kr, K_BLOCK_COLS))
                       .reshape(N_COL_BLOCKS, Q_BLOCK_COLS, n_loc))
    dc = np.clip(kc - q_cols[..., None], -(WIN_COLS - 1), WIN_COLS - 1) + WIN_COLS - 1
    col_bias = rpb[:, :, dc]

    def row(r):
        rs = jnp.clip(r - kr // 2, 0, rows - kr)
        k_rows = lax.dynamic_slice_in_dim(kg, rs, kr, axis=1)
        v_rows = lax.dynamic_slice_in_dim(vg, rs, kr, axis=1)
        k_loc = jnp.take(k_rows, k_cols, axis=2).transpose(0, 2, 1, 3, 4, 5).reshape(
            b, N_COL_BLOCKS, n_loc, N_HEADS, HEAD_DIM)
        v_loc = jnp.take(v_rows, k_cols, axis=2).transpose(0, 2, 1, 3, 4, 5).reshape(
            b, N_COL_BLOCKS, n_loc, N_HEADS, HEAD_DIM)
        dr = rs + jnp.arange(kr) - r + WIN_ROWS - 1
        bias = jnp.take(col_bias, dr, axis=1).transpose(0, 2, 3, 1, 4).reshape(
            N_HEADS, N_COL_BLOCKS, Q_BLOCK_COLS, n_loc)
        qr = lax.dynamic_index_in_dim(qg, r, axis=1, keepdims=False)
        s_loc = jnp.einsum('bjqhd,bjkhd->bhjqk', qr, k_loc).astype(jnp.float32) * scale \
            + bias.astype(jnp.float32)
        s_loc = jnp.where(mask, s_loc, NEG_INF)
        s_ctx = jnp.einsum('bjqhd,bphd->bhjqp', qr, k_ctx).astype(jnp.float32) * scale
        p = jax.nn.softmax(jnp.concatenate([s_loc, s_ctx], axis=-1), axis=-1).astype(v.dtype)
        o = jnp.einsum('bhjqk,bjkhd->bjqhd', p[..., :n_loc], v_loc) \
            + jnp.einsum('bhjqp,bphd->bjqhd', p[..., n_loc:], v_ctx)
        return o.reshape(b, GRID_W, N_HEADS, HEAD_DIM)

    o = lax.map(row, jnp.arange(rows))
    return o.transpose(1, 0, 2, 3, 4).reshape(b, t, D_C)


def _layer(x, mods, p, ctx_kv):
    sh1, sc1, g1, sh2, sc2, g2, sh3, sc3, g3 = mods
    u = _rms(x, p['g_ff1']) * (1 + sc1) + sh1
    x = x + 0.5 * g1 * _swiglu(u, p['w_ff1_gate'], p['w_ff1_up'], p['w_ff1_down'])

    u = _rms(x, p['g_mix']) * (1 + sc2) + sh2
    a_val, a_gate, b_g, c_g, h_b, q, k, v, ga, gb, gc = _project(u, p['w_in'])
    ha = _dwconv(a_val * jax.nn.sigmoid(a_gate), p['conv_a_w'], p['conv_a_b'])
    ya = jax.nn.silu(_layer_norm(ha, p['ln_a_g'], p['ln_a_b'])) @ p['w_a_out']
    yb = (b_g * _dwconv(c_g * h_b, p['conv_b_w'])) @ p['w_b_out']
    q = _rms(_heads(q), p['q_norm_g'])
    k = _rms(_heads(k), p['k_norm_g'])
    v = _heads(v)
    if ctx_kv is None:
        o = _context_attention(q, k, v)
        new_kv = (k, v)
    else:
        o = _neighbourhood_attention(q, k, v, ctx_kv[0], ctx_kv[1], p['rpb'])
        new_kv = None
    yc = o @ p['w_c_out']
    m = jax.nn.sigmoid(ga) * ya + jax.nn.sigmoid(gb) * yb + jax.nn.sigmoid(gc) * yc
    x = x + g2 * (m @ p['w_merge'])

    u = _rms(x, p['g_ff2']) * (1 + sc3) + sh3
    x = x + 0.5 * g3 * _swiglu(u, p['w_ff2_gate'], p['w_ff2_up'], p['w_ff2_down'])
    return x, new_kv


def setup_inputs(seed: int = 0) -> dict:
    key = jax.random.key(seed)
    ks = jax.random.split(key, 32)
    f32 = jnp.float32

    def nrm(k, shape, scale):
        return jax.random.normal(k, shape, f32) * scale

    L, D = DEPTH, D_MODEL
    return {
        "x_prompt": nrm(ks[0], (BATCH, SEQ, D), 1.0),
        "x_sample": nrm(ks[1], (DEC_BATCH, DEC_SEQ, D), 1.0),
        "cache_k": nrm(ks[2], (DEC_BATCH, DEPTH, PAST_LEN, N_HEADS, HEAD_DIM), 1.0),
        "cache_v": nrm(ks[3], (DEC_BATCH, DEPTH, PAST_LEN, N_HEADS, HEAD_DIM), 1.0),
        "c": nrm(ks[4], (DEC_BATCH, D), 1.0),
        "c_ctx": nrm(ks[5], (D,), 1.0),
        "w_ada": nrm(ks[6], (L, D, N_MOD * D), 0.5 * D ** -0.5),
        "b_ada": nrm(ks[7], (L, N_MOD * D), 0.02),
        "g_ff1": 1.0 + nrm(ks[8], (L, D), 0.02),
        "w_ff1_gate": nrm(ks[9], (L, D, D_FF), D ** -0.5),
        "w_ff1_up": nrm(ks[10], (L, D, D_FF), D ** -0.5),
        "w_ff1_down": nrm(ks[11], (L, D_FF, D), D_FF ** -0.5),
        "g_mix": 1.0 + nrm(ks[12], (L, D), 0.02),
        "w_in": nrm(ks[13], (L, D, N_IN), D ** -0.5),
        "conv_a_w": nrm(ks[14], (L, CONV_A, D_A), CONV_A ** -0.5),
        "conv_a_b": nrm(ks[15], (L, D_A), 0.02),
        "ln_a_g": 1.0 + nrm(ks[16], (L, D_A), 0.02),
        "ln_a_b": nrm(ks[17], (L, D_A), 0.02),
        "w_a_out": nrm(ks[18], (L, D_A, D), D_A ** -0.5),
        "conv_b_w": nrm(ks[19], (L, CONV_B, D_B), CONV_B ** -0.5),
        "w_b_out": nrm(ks[20], (L, D_B, D), D_B ** -0.5),
        "q_norm_g": 1.0 + nrm(ks[21], (L, HEAD_DIM), 0.02),
        "k_norm_g": 1.0 + nrm(ks[22], (L, HEAD_DIM), 0.02),
        "rpb": nrm(ks[23], (L, N_HEADS, 2 * WIN_ROWS - 1, 2 * WIN_COLS - 1), 0.1),
        "w_c_out": nrm(ks[24], (L, D_C, D), D_C ** -0.5),
        "w_merge": nrm(ks[25], (L, D, D), D ** -0.5),
        "g_ff2": 1.0 + nrm(ks[26], (L, D), 0.02),
        "w_ff2_gate": nrm(ks[27], (L, D, D_FF), D ** -0.5),
        "w_ff2_up": nrm(ks[28], (L, D, D_FF), D ** -0.5),
        "w_ff2_down": nrm(ks[29], (L, D_FF, D), D_FF ** -0.5),
    }


def reference(x_prompt, x_sample, cache_k, cache_v, c, c_ctx, w_ada, b_ada,
              g_ff1, w_ff1_gate, w_ff1_up, w_ff1_down, g_mix, w_in,
              conv_a_w, conv_a_b, ln_a_g, ln_a_b, w_a_out, conv_b_w, w_b_out,
              q_norm_g, k_norm_g, rpb, w_c_out, w_merge,
              g_ff2, w_ff2_gate, w_ff2_up, w_ff2_down):
    h_ctx = x_prompt
    h_lat = x_sample
    new_ks = []
    new_vs = []
    for l in range(DEPTH):
        p = {
            'g_ff1': g_ff1[l], 'w_ff1_gate': w_ff1_gate[l], 'w_ff1_up': w_ff1_up[l],
            'w_ff1_down': w_ff1_down[l], 'g_mix': g_mix[l], 'w_in': w_in[l],
            'conv_a_w': conv_a_w[l], 'conv_a_b': conv_a_b[l], 'ln_a_g': ln_a_g[l],
            'ln_a_b': ln_a_b[l], 'w_a_out': w_a_out[l], 'conv_b_w': conv_b_w[l],
            'w_b_out': w_b_out[l], 'q_norm_g': q_norm_g[l], 'k_norm_g': k_norm_g[l],
            'rpb': rpb[l], 'w_c_out': w_c_out[l], 'w_merge': w_merge[l],
            'g_ff2': g_ff2[l], 'w_ff2_gate': w_ff2_gate[l], 'w_ff2_up': w_ff2_up[l],
            'w_ff2_down': w_ff2_down[l],
        }
        mods_ctx = _modulation(c_ctx[None, :], w_ada[l], b_ada[l])
        mods_lat = _modulation(c, w_ada[l], b_ada[l])
        h_ctx, kv = _layer(h_ctx, mods_ctx, p, None)
        new_ks.append(kv[0])
        new_vs.append(kv[1])
        h_lat, _ = _layer(h_lat, mods_lat, p, (cache_k[:, l], cache_v[:, l]))
    new_k = jnp.stack(new_ks, axis=1)
    new_v = jnp.stack(new_vs, axis=1)
    return (h_ctx, h_lat, new_k, new_v)
```

```python
from functools import partial

import numpy as np
import jax
import jax.numpy as jnp
from jax import lax
from jax.experimental import pallas as pl
from jax.experimental.pallas import tpu as pltpu

D_MODEL = 1024
DEPTH = 2
GRID_W = 64
D_A = 512
CONV_A = 31
D_B = 512
CONV_B = 3
N_HEADS = 8
HEAD_DIM = 64
D_C = N_HEADS * HEAD_DIM
WIN_ROWS = 8
WIN_COLS = 16
D_FF = 2816
N_MOD = 9
N_IN = 2 * D_A + 3 * D_B + 3 * D_C + 3 * D_MODEL
EPS = 1e-6
NEG_INF = -1e30

LANES = 128
HEADS_PER_LANE_TILE = LANES // HEAD_DIM
HALO = 16
MOD_ROWS = 16
VMEM_LIMIT = 58 * 1024 * 1024
TM = 512
ROWS_PER_TILE = 8

F32 = jnp.float32
BF16 = jnp.bfloat16


def _dot(a, b):
    return jnp.dot(a, b, preferred_element_type=F32)


def _dot_t(a, b):
    return lax.dot_general(a, b, (((1,), (1,)), ((), ())), preferred_element_type=F32)


def _resident(shape):
    zeros = (0,) * len(shape)
    return pl.BlockSpec(shape, lambda *_: zeros, pipeline_mode=pl.Buffered(1))


def _params(n_grid_dims):
    return pltpu.CompilerParams(dimension_semantics=("arbitrary",) * n_grid_dims,
                                vmem_limit_bytes=VMEM_LIMIT)


def _ada_ln(x, g, shift, scale):
    y = x * lax.rsqrt(jnp.mean(x * x, axis=-1, keepdims=True) + EPS)
    return (y * g) * (1.0 + scale) + shift


def _mods_kernel(c_ref, w_ref, b_ref, o_ref):
    c = c_ref[...]
    s = (c * jax.nn.sigmoid(c)).astype(BF16)
    o_ref[...] = _dot(s, w_ref[...].astype(BF16)) + b_ref[...]


def _modulation(cvecs, w_ada, b_ada):
    out = pl.pallas_call(
        _mods_kernel,
        grid=(DEPTH, N_MOD),
        in_specs=[pl.BlockSpec((MOD_ROWS, D_MODEL), lambda l, j: (0, 0)),
                  pl.BlockSpec((None, D_MODEL, D_MODEL), lambda l, j: (l, 0, j)),
                  pl.BlockSpec((None, 1, D_MODEL), lambda l, j: (l, 0, j))],
        out_specs=pl.BlockSpec((None, MOD_ROWS, D_MODEL), lambda l, j: (l, 0, j)),
        out_shape=jax.ShapeDtypeStruct((DEPTH, MOD_ROWS, N_MOD * D_MODEL), F32),
        compiler_params=_params(2),
        name="modulation",
    )(cvecs, w_ada, b_ada.reshape(DEPTH, 1, N_MOD * D_MODEL))
    return out.reshape(DEPTH, MOD_ROWS, N_MOD, D_MODEL)


def _ffn_kernel(mod0, x_ref, m_ref, g_ref, wg_ref, wu_ref, wd_ref, o_ref):
    x = x_ref[...]
    shift, scale, gate = (m_ref[mod0 + i:mod0 + i + 1, :] for i in range(3))
    u = _ada_ln(x, g_ref[...], shift, scale).astype(BF16)
    a = _dot(u, wg_ref[...])
    b = _dot(u, wu_ref[...])
    h = (a * jax.nn.sigmoid(a) * b).astype(BF16)
    o_ref[...] = x + (0.5 * gate) * _dot(h, wd_ref[...])


def _ffn(x, mods, mod0, row_of_tile, g, wg, wu, wd):
    n = x.shape[0]
    return pl.pallas_call(
        partial(_ffn_kernel, mod0),
        grid=(n // TM,),
        in_specs=[pl.BlockSpec((TM, D_MODEL), lambda i: (i, 0)),
                  pl.BlockSpec((None, N_MOD, D_MODEL), lambda i: (row_of_tile(i), 0, 0)),
                  _resident((1, D_MODEL)),
                  _resident((D_MODEL, D_FF)), _resident((D_MODEL, D_FF)), _resident((D_FF, D_MODEL))],
        out_specs=pl.BlockSpec((TM, D_MODEL), lambda i: (i, 0)),
        out_shape=jax.ShapeDtypeStruct((n, D_MODEL), F32),
        compiler_params=_params(1),
        name="ffn",
    )(x, mods, g, wg, wu, wd)


def _head_rms(t, bd, g):
    ms = _dot((t * t).astype(BF16), bd)
    return t * lax.rsqrt(ms + EPS) * g


def _proj_kernel(emit_kv, x_ref, m_ref, g_ref, w_ref, bd_ref, qg_ref, kg_ref, *out_refs):
    a_ref, bg_ref, ch_ref, q_ref, k_ref, v_ref, gate_ref = out_refs[:7]
    shift, scale = m_ref[3:4, :], m_ref[4:5, :]
    u = _ada_ln(x_ref[...], g_ref[...], shift, scale).astype(BF16)

    def seg(start, width):
        return _dot(u, w_ref[:, start:start + width])

    o = 0
    a_ref[...] = (seg(o, D_A) * jax.nn.sigmoid(seg(o + D_A, D_A))).astype(BF16)
    o += 2 * D_A
    bg_ref[...] = seg(o, D_B).astype(BF16)
    ch_ref[...] = (seg(o + D_B, D_B) * seg(o + 2 * D_B, D_B)).astype(BF16)
    o += 3 * D_B
    bd = bd_ref[...]
    q_ref[...] = _head_rms(seg(o, D_C), bd, qg_ref[...]).astype(BF16)
    k = _head_rms(seg(o + D_C, D_C), bd, kg_ref[...])
    v = seg(o + 2 * D_C, D_C)
    k_ref[...] = k.astype(BF16)
    v_ref[...] = v.astype(BF16)
    if emit_kv:
        out_refs[7][...] = k
        out_refs[8][...] = v
    o += 3 * D_C
    for j in range(3):
        gate_ref[:, j * D_MODEL:(j + 1) * D_MODEL] = jax.nn.sigmoid(seg(o + j * D_MODEL, D_MODEL)).astype(BF16)


def _proj(x, mods, row_of_tile, g, w_in, bd, qg, kg, emit_kv):
    n = x.shape[0]
    tok = lambda width: pl.BlockSpec((TM, width), lambda i: (i, 0))
    widths = [D_A, D_B, D_B, D_C, D_C, D_C, 3 * D_MODEL]
    out_shape = [jax.ShapeDtypeStruct((n, w), BF16) for w in widths]
    out_specs = [tok(w) for w in widths]
    if emit_kv:
        out_shape += [jax.ShapeDtypeStruct((n, D_C), F32)] * 2
        out_specs += [tok(D_C)] * 2
    return pl.pallas_call(
        partial(_proj_kernel, emit_kv),
        grid=(n // TM,),
        in_specs=[tok(D_MODEL),
                  pl.BlockSpec((None, N_MOD, D_MODEL), lambda i: (row_of_tile(i), 0, 0)),
                  _resident((1, D_MODEL)), _resident((D_MODEL, N_IN)), _resident((D_C, D_C)),
                  _resident((1, D_C)), _resident((1, D_C))],
        out_specs=out_specs,
        out_shape=out_shape,
        compiler_params=_params(1),
        name="proj",
    )(x, mods, g, w_in, bd, qg, kg)


def _conv_branches(t, apad, cpad, bg, caw_ref, cab_ref, lng_ref, lnb_ref, cbw_ref, wa_ref, wb_ref):
    ha = cab_ref[...] + apad[pl.ds(HALO - CONV_A // 2, t), :] * caw_ref[0:1, :]
    for j in range(1, CONV_A):
        ha = ha + apad[pl.ds(HALO - CONV_A // 2 + j, t), :] * caw_ref[j:j + 1, :]
    mu = jnp.mean(ha, axis=-1, keepdims=True)
    hc = ha - mu
    ln = hc * lax.rsqrt(jnp.mean(hc * hc, axis=-1, keepdims=True) + EPS) * lng_ref[...] + lnb_ref[...]
    ya = _dot((ln * jax.nn.sigmoid(ln)).astype(BF16), wa_ref[...])
    hb = cpad[pl.ds(HALO - 1, t), :] * cbw_ref[0:1, :]
    for j in range(1, CONV_B):
        hb = hb + cpad[pl.ds(HALO - 1 + j, t), :] * cbw_ref[j:j + 1, :]
    yb = _dot((bg.astype(F32) * hb).astype(BF16), wb_ref[...])
    return ya, yb


def _merge(x_ref, gate_ref, g2, ya, yb, yc, wm_ref, o_ref):
    m = (gate_ref[:, 0:D_MODEL].astype(F32) * ya
         + gate_ref[:, D_MODEL:2 * D_MODEL].astype(F32) * yb
         + gate_ref[:, 2 * D_MODEL:3 * D_MODEL].astype(F32) * yc)
    o_ref[...] = x_ref[...] + g2 * _dot(m.astype(BF16), wm_ref[...])


def _head_lane_masks():
    lane = lax.broadcasted_iota(jnp.int32, (1, LANES), 1)
    return [(lane >= e * HEAD_DIM) & (lane < (e + 1) * HEAD_DIM) for e in range(HEADS_PER_LANE_TILE)]


def _mix_ctx_kernel(x_ref, m_ref, a_ref, bg_ref, ch_ref, q_ref, k_ref, v_ref, gate_ref,
                    caw_ref, cab_ref, lng_ref, lnb_ref, cbw_ref, wa_ref, wb_ref, wc_ref, wm_ref,
                    o_ref, apad, cpad):
    t = x_ref.shape[0]
    zero_halo = jnp.zeros((HALO, D_A), F32)
    for pad, src in ((apad, a_ref), (cpad, ch_ref)):
        pad[0:HALO, :] = zero_halo
        pad[HALO:HALO + t, :] = src[...].astype(F32)
        pad[HALO + t:HALO + t + HALO, :] = zero_halo
    ya, yb = _conv_branches(t, apad, cpad, bg_ref[...], caw_ref, cab_ref, lng_ref, lnb_ref,
                            cbw_ref, wa_ref, wb_ref)
    masks = _head_lane_masks()
    o_tiles = []
    for mt in range(D_C // LANES):
        lanes = slice(mt * LANES, (mt + 1) * LANES)
        q2, k2, v2 = q_ref[:, lanes], k_ref[:, lanes], v_ref[:, lanes]
        o2 = None
        for e in range(HEADS_PER_LANE_TILE):
            s = _dot_t(jnp.where(masks[e], q2, jnp.zeros_like(q2)), k2)
            p = jnp.exp(s - jnp.max(s, axis=-1, keepdims=True))
            p = p * (1.0 / jnp.sum(p, axis=-1, keepdims=True))
            oe = _dot(p.astype(BF16), v2)
            o2 = oe if o2 is None else jnp.where(masks[e], oe, o2)
        o_tiles.append(o2)
    yc = _dot(jnp.concatenate(o_tiles, axis=1).astype(BF16), wc_ref[...])
    _merge(x_ref, gate_ref, m_ref[5:6, :], ya, yb, yc, wm_ref, o_ref)


def _mix_lat_kernel(x_ref, m_ref, a_ref, ap_ref, an_ref, bg_ref, ch_ref, cp_ref, cn_ref,
                    q_ref, k_ref, v_ref, kc_ref, vc_ref, bias_ref, gate_ref,
                    caw_ref, cab_ref, lng_ref, lnb_ref, cbw_ref, wa_ref, wb_ref, wc_ref, wm_ref,
                    o_ref, apad, cpad):
    t = x_ref.shape[0]
    i = pl.program_id(1)
    n_tiles = pl.num_programs(1)
    rows = n_tiles * ROWS_PER_TILE
    has_prev = (i > 0).astype(F32)
    has_next = (i < n_tiles - 1).astype(F32)
    for pad, prev, cur, nxt in ((apad, ap_ref, a_ref, an_ref), (cpad, cp_ref, ch_ref, cn_ref)):
        pad[0:HALO, :] = prev[...].astype(F32) * has_prev
        pad[HALO:HALO + t, :] = cur[...].astype(F32)
        pad[HALO + t:HALO + t + HALO, :] = nxt[...].astype(F32) * has_next
    ya, yb = _conv_branches(t, apad, cpad, bg_ref[...], caw_ref, cab_ref, lng_ref, lnb_ref,
                            cbw_ref, wa_ref, wb_ref)

    masks = _head_lane_masks()
    n_win = WIN_ROWS * GRID_W
    o_tiles = []
    for mt in range(D_C // LANES):
        lanes = slice(mt * LANES, (mt + 1) * LANES)
        q2 = q_ref[:, lanes]
        kc2 = kc_ref[:, lanes].astype(BF16)
        vc2 = vc_ref[:, lanes].astype(BF16)
        o2 = None
        for e in range(HEADS_PER_LANE_TILE):
            h = mt * HEADS_PER_LANE_TILE + e
            qm = jnp.where(masks[e], q2, jnp.zeros_like(q2))
            s_ctx = _dot_t(qm, kc2)
            o_loc, p_ctx = [], []
            for rr in range(ROWS_PER_TILE):
                r = i * ROWS_PER_TILE + rr
                rs = jnp.clip(r - WIN_ROWS // 2, 0, rows - WIN_ROWS)
                start = pl.multiple_of(rs * GRID_W, GRID_W)
                rows_q = slice(rr * GRID_W, (rr + 1) * GRID_W)
                s_loc = _dot_t(qm[rows_q], k_ref[pl.ds(start, n_win), lanes]) + bias_ref[h, r - rs]
                s_c = s_ctx[rows_q]
                mx = jnp.maximum(jnp.max(s_loc, axis=-1, keepdims=True),
                                 jnp.max(s_c, axis=-1, keepdims=True))
                p_l = jnp.exp(s_loc - mx)
                p_c = jnp.exp(s_c - mx)
                inv = 1.0 / (jnp.sum(p_l, axis=-1, keepdims=True) + jnp.sum(p_c, axis=-1, keepdims=True))
                o_loc.append(_dot((p_l * inv).astype(BF16), v_ref[pl.ds(start, n_win), lanes]))
                p_ctx.append((p_c * inv).astype(BF16))
            oe = jnp.concatenate(o_loc, axis=0) + _dot(jnp.concatenate(p_ctx, axis=0), vc2)
            o2 = oe if o2 is None else jnp.where(masks[e], oe, o2)
        o_tiles.append(o2)
    yc = _dot(jnp.concatenate(o_tiles, axis=1).astype(BF16), wc_ref[...])
    _merge(x_ref, gate_ref, m_ref[5:6, :], ya, yb, yc, wm_ref, o_ref)


def _mixer_weight_specs():
    return [_resident((CONV_A, D_A)), _resident((1, D_A)), _resident((1, D_A)), _resident((1, D_A)),
            _resident((CONV_B, D_B)), _resident((D_A, D_MODEL)), _resident((D_B, D_MODEL)),
            _resident((D_C, D_MODEL)), _resident((D_MODEL, D_MODEL))]


def _mix_ctx(x, mods, pieces, weights, seq):
    n = x.shape[0]
    a, bg, ch, q, k, v, gates = pieces
    tok = lambda width: pl.BlockSpec((seq, width), lambda i: (i, 0))
    return pl.pallas_call(
        _mix_ctx_kernel,
        grid=(n // seq,),
        in_specs=[tok(D_MODEL), pl.BlockSpec((None, N_MOD, D_MODEL), lambda i: (0, 0, 0)),
                  tok(D_A), tok(D_B), tok(D_B), tok(D_C), tok(D_C), tok(D_C), tok(3 * D_MODEL)]
                 + _mixer_weight_specs(),
        out_specs=tok(D_MODEL),
        out_shape=jax.ShapeDtypeStruct((n, D_MODEL), F32),
        scratch_shapes=[pltpu.VMEM((seq + 2 * HALO, D_A), F32), pltpu.VMEM((seq + 2 * HALO, D_B), F32)],
        compiler_params=_params(1),
        name="mix_ctx",
    )(x, mods, a, bg, ch, q, k, v, gates, *weights)


def _mix_lat(x, mods, pieces, cache_k, cache_v, layer, bias, weights, batch, seq):
    a, bg, ch, q, k, v, gates = (p.reshape(batch, seq, p.shape[-1]) for p in pieces)
    t = ROWS_PER_TILE * GRID_W
    halo_per_tile = t // HALO
    n_halo = seq // HALO
    past = cache_k.shape[2]
    tok = lambda width: pl.BlockSpec((None, t, width), lambda b, i: (b, i, 0))
    prev = lambda width: pl.BlockSpec((None, HALO, width),
                                      lambda b, i: (b, jnp.maximum(i * halo_per_tile - 1, 0), 0))
    nxt = lambda width: pl.BlockSpec((None, HALO, width),
                                     lambda b, i: (b, jnp.minimum((i + 1) * halo_per_tile, n_halo - 1), 0))
    seq_resident = pl.BlockSpec((None, seq, D_C), lambda b, i: (b, 0, 0), pipeline_mode=pl.Buffered(1))
    cache = pl.BlockSpec((None, None, past, D_C), lambda b, i: (b, layer, 0, 0))
    out = pl.pallas_call(
        _mix_lat_kernel,
        grid=(batch, seq // t),
        in_specs=[tok(D_MODEL), pl.BlockSpec((None, N_MOD, D_MODEL), lambda b, i: (1 + b, 0, 0)),
                  tok(D_A), prev(D_A), nxt(D_A), tok(D_B), tok(D_B), prev(D_B), nxt(D_B),
                  tok(D_C), seq_resident, seq_resident, cache, cache,
                  _resident((N_HEADS, WIN_ROWS, GRID_W, WIN_ROWS * GRID_W)), tok(3 * D_MODEL)]
                 + _mixer_weight_specs(),
        out_specs=tok(D_MODEL),
        out_shape=jax.ShapeDtypeStruct((batch, seq, D_MODEL), F32),
        scratch_shapes=[pltpu.VMEM((t + 2 * HALO, D_A), F32), pltpu.VMEM((t + 2 * HALO, D_B), F32)],
        compiler_params=_params(2),
        name="mix_lat",
    )(x.reshape(batch, seq, D_MODEL), mods, a, a, a, bg, ch, ch, ch, q, k, v,
      cache_k, cache_v, bias, gates, *weights)
    return out.reshape(batch * seq, D_MODEL)


def _local_bias_table(rpb):
    qc = np.arange(GRID_W)[:, None]
    kc = np.arange(GRID_W)[None, :]
    win_start = np.clip(qc - WIN_COLS // 2, 0, GRID_W - WIN_COLS)
    valid = (kc >= win_start) & (kc < win_start + WIN_COLS)
    dc = np.clip(kc - qc, -(WIN_COLS - 1), WIN_COLS - 1) + WIN_COLS - 1
    col_bias = jnp.where(jnp.asarray(valid)[None, None], rpb[:, :, dc], NEG_INF)
    per_delta = [col_bias[:, WIN_ROWS - 1 - d:2 * WIN_ROWS - 1 - d] for d in range(WIN_ROWS)]
    tab = jnp.stack(per_delta, axis=1)
    return tab.transpose(0, 1, 3, 2, 4).reshape(N_HEADS, WIN_ROWS, GRID_W, WIN_ROWS * GRID_W)


def kernel(x_prompt, x_sample, cache_k, cache_v, c, c_ctx, w_ada, b_ada, g_ff1, w_ff1_gate, w_ff1_up,
           w_ff1_down, g_mix, w_in, conv_a_w, conv_a_b, ln_a_g, ln_a_b, w_a_out, conv_b_w, w_b_out,
           q_norm_g, k_norm_g, rpb, w_c_out, w_merge, g_ff2, w_ff2_gate, w_ff2_up, w_ff2_down):
    batch, seq, _ = x_prompt.shape
    dec_batch, dec_seq, _ = x_sample.shape
    past = cache_k.shape[2]
    assert dec_batch + 1 <= MOD_ROWS and dec_seq % (ROWS_PER_TILE * GRID_W) == 0
    assert (batch * seq) % TM == 0 and dec_seq % TM == 0 and seq % HALO == 0

    cvecs = jnp.zeros((MOD_ROWS, D_MODEL), F32).at[0].set(c_ctx).at[1:1 + dec_batch].set(c)
    mods = _modulation(cvecs, w_ada, b_ada)
    ctx_row = lambda i: 0
    lat_row = lambda i: 1 + (i * TM) // dec_seq

    head = np.arange(D_C) // HEAD_DIM
    bd = jnp.asarray((head[:, None] == head[None, :]) / HEAD_DIM, BF16)
    ck = cache_k.reshape(dec_batch, DEPTH, past, D_C)
    cv = cache_v.reshape(dec_batch, DEPTH, past, D_C)

    h_ctx = x_prompt.reshape(batch * seq, D_MODEL)
    h_lat = x_sample.reshape(dec_batch * dec_seq, D_MODEL)
    new_ks, new_vs = [], []
    row = lambda v: v.reshape(1, -1)
    for l in range(DEPTH):
        ff1 = (row(g_ff1[l]), w_ff1_gate[l].astype(BF16), w_ff1_up[l].astype(BF16), w_ff1_down[l].astype(BF16))
        ff2 = (row(g_ff2[l]), w_ff2_gate[l].astype(BF16), w_ff2_up[l].astype(BF16), w_ff2_down[l].astype(BF16))
        qg = row(jnp.tile(q_norm_g[l], N_HEADS) * HEAD_DIM ** -0.5)
        kg = row(jnp.tile(k_norm_g[l], N_HEADS))
        proj_w = (row(g_mix[l]), w_in[l].astype(BF16), bd, qg, kg)
        mix_w = (conv_a_w[l], row(conv_a_b[l]), row(ln_a_g[l]), row(ln_a_b[l]), conv_b_w[l],
                 w_a_out[l].astype(BF16), w_b_out[l].astype(BF16), w_c_out[l].astype(BF16),
                 w_merge[l].astype(BF16))
        bias = _local_bias_table(rpb[l])

        h_ctx = _ffn(h_ctx, mods[l], 0, ctx_row, *ff1)
        *pieces, k_new, v_new = _proj(h_ctx, mods[l], ctx_row, *proj_w, emit_kv=True)
        new_ks.append(k_new.reshape(batch, seq, N_HEADS, HEAD_DIM))
        new_vs.append(v_new.reshape(batch, seq, N_HEADS, HEAD_DIM))
        h_ctx = _mix_ctx(h_ctx, mods[l], pieces, mix_w, seq)
        h_ctx = _ffn(h_ctx, mods[l], 6, ctx_row, *ff2)

        h_lat = _ffn(h_lat, mods[l], 0, lat_row, *ff1)
        pieces = _proj(h_lat, mods[l], lat_row, *proj_w, emit_kv=False)
        h_lat = _mix_lat(h_lat, mods[l], pieces, ck, cv, l, bias, mix_w, dec_batch, dec_seq)
        h_lat = _ffn(h_lat, mods[l], 6, lat_row, *ff2)

    return (h_ctx.reshape(batch, seq, D_MODEL), h_lat.reshape(dec_batch, dec_seq, D_MODEL),
            jnp.stack(new_ks, axis=1), jnp.stack(new_vs, axis=1))
```

```python
from functools import partial

import numpy as np
import jax
import jax.numpy as jnp
from jax import lax
from jax.experimental import pallas as pl
from jax.experimental.pallas import tpu as pltpu

D_MODEL = 1024
DEPTH = 2
GRID_W = 64
D_A = 512
CONV_A = 31
D_B = 512
CONV_B = 3
N_HEADS = 8
HEAD_DIM = 64
D_C = N_HEADS * HEAD_DIM
WIN_ROWS = 8
WIN_COLS = 16
D_FF = 2816
N_MOD = 9
N_IN = 2 * D_A + 3 * D_B + 3 * D_C + 3 * D_MODEL
EPS = 1e-6
NEG_INF = -1e30

LANES = 128
SUBLANES = 8
HEADS_PER_LANE_TILE = LANES // HEAD_DIM
HALO = 16
MOD_ROWS = 16
VMEM_LIMIT = 58 * 1024 * 1024
TM = 512
ROWS_PER_TILE = 8

F32 = jnp.float32
BF16 = jnp.bfloat16


def _dot(a, b):
    return jnp.dot(a, b, preferred_element_type=F32)


def _dot_t(a, b):
    return lax.dot_general(a, b, (((1,), (1,)), ((), ())), preferred_element_type=F32)


def _resident(shape):
    zeros = (0,) * len(shape)
    return pl.BlockSpec(shape, lambda *_: zeros, pipeline_mode=pl.Buffered(1))


def _params(n_grid_dims):
    return pltpu.CompilerParams(dimension_semantics=("arbitrary",) * n_grid_dims,
                                vmem_limit_bytes=VMEM_LIMIT)


def _ada_ln(x, g, shift, scale):
    y = x * lax.rsqrt(jnp.mean(x * x, axis=-1, keepdims=True) + EPS)
    return (y * g) * (1.0 + scale) + shift


def _mods_kernel(c_ref, w_ref, b_ref, o_ref):
    c = c_ref[...]
    s = (c * jax.nn.sigmoid(c)).astype(BF16)
    o_ref[...] = _dot(s, w_ref[...].astype(BF16)) + b_ref[...]


def _modulation(cvecs, w_ada, b_ada):
    out = pl.pallas_call(
        _mods_kernel,
        grid=(DEPTH, N_MOD),
        in_specs=[pl.BlockSpec((MOD_ROWS, D_MODEL), lambda l, j: (0, 0)),
                  pl.BlockSpec((None, D_MODEL, D_MODEL), lambda l, j: (l, 0, j)),
                  pl.BlockSpec((None, 1, D_MODEL), lambda l, j: (l, 0, j))],
        out_specs=pl.BlockSpec((None, MOD_ROWS, D_MODEL), lambda l, j: (l, 0, j)),
        out_shape=jax.ShapeDtypeStruct((DEPTH, MOD_ROWS, N_MOD * D_MODEL), F32),
        compiler_params=_params(2),
        name="modulation",
    )(cvecs, w_ada, b_ada.reshape(DEPTH, 1, N_MOD * D_MODEL))
    return out.reshape(DEPTH, MOD_ROWS, N_MOD, D_MODEL)


def _ffn_kernel(mod0, x_ref, m_ref, g_ref, wg_ref, wu_ref, wd_ref, o_ref):
    x = x_ref[...]
    shift, scale, gate = (m_ref[mod0 + i:mod0 + i + 1, :] for i in range(3))
    u = _ada_ln(x, g_ref[...], shift, scale).astype(BF16)
    a = _dot(u, wg_ref[...])
    b = _dot(u, wu_ref[...])
    h = (a * jax.nn.sigmoid(a) * b).astype(BF16)
    o_ref[...] = x + (0.5 * gate) * _dot(h, wd_ref[...])


def _ffn(x, mods, mod0, row_of_tile, g, wg, wu, wd):
    n = x.shape[0]
    return pl.pallas_call(
        partial(_ffn_kernel, mod0),
        grid=(n // TM,),
        in_specs=[pl.BlockSpec((TM, D_MODEL), lambda i: (i, 0)),
                  pl.BlockSpec((None, N_MOD, D_MODEL), lambda i: (row_of_tile(i), 0, 0)),
                  _resident((1, D_MODEL)),
                  _resident((D_MODEL, D_FF)), _resident((D_MODEL, D_FF)), _resident((D_FF, D_MODEL))],
        out_specs=pl.BlockSpec((TM, D_MODEL), lambda i: (i, 0)),
        out_shape=jax.ShapeDtypeStruct((n, D_MODEL), F32),
        compiler_params=_params(1),
        name="ffn",
    )(x, mods, g, wg, wu, wd)


def _head_rms(t, bd, g):
    ms = _dot((t * t).astype(BF16), bd)
    return t * lax.rsqrt(ms + EPS) * g


def _proj_kernel(emit_kv, x_ref, m_ref, g_ref, w_ref, bd_ref, qg_ref, kg_ref, *out_refs):
    a_ref, bg_ref, ch_ref, q_ref, k_ref, v_ref, gate_ref = out_refs[:7]
    shift, scale = m_ref[3:4, :], m_ref[4:5, :]
    u = _ada_ln(x_ref[...], g_ref[...], shift, scale).astype(BF16)

    def seg(start, width):
        return _dot(u, w_ref[:, start:start + width])

    o = 0
    a_ref[...] = (seg(o, D_A) * jax.nn.sigmoid(seg(o + D_A, D_A))).astype(BF16)
    o += 2 * D_A
    bg_ref[...] = seg(o, D_B).astype(BF16)
    ch_ref[...] = (seg(o + D_B, D_B) * seg(o + 2 * D_B, D_B)).astype(BF16)
    o += 3 * D_B
    bd = bd_ref[...]
    q_ref[...] = _head_rms(seg(o, D_C), bd, qg_ref[...]).astype(BF16)
    k = _head_rms(seg(o + D_C, D_C), bd, kg_ref[...])
    v = seg(o + 2 * D_C, D_C)
    k_ref[...] = k.astype(BF16)
    v_ref[...] = v.astype(BF16)
    if emit_kv:
        out_refs[7][...] = k
        out_refs[8][...] = v
    o += 3 * D_C
    for j in range(3):
        gate_ref[:, j * D_MODEL:(j + 1) * D_MODEL] = jax.nn.sigmoid(seg(o + j * D_MODEL, D_MODEL)).astype(BF16)


def _proj(x, mods, row_of_tile, g, w_in, bd, qg, kg, emit_kv):
    n = x.shape[0]
    tok = lambda width: pl.BlockSpec((TM, width), lambda i: (i, 0))
    widths = [D_A, D_B, D_B, D_C, D_C, D_C, 3 * D_MODEL]
    out_shape = [jax.ShapeDtypeStruct((n, w), BF16) for w in widths]
    out_specs = [tok(w) for w in widths]
    if emit_kv:
        out_shape += [jax.ShapeDtypeStruct((n, D_C), F32)] * 2
        out_specs += [tok(D_C)] * 2
    return pl.pallas_call(
        partial(_proj_kernel, emit_kv),
        grid=(n // TM,),
        in_specs=[tok(D_MODEL),
                  pl.BlockSpec((None, N_MOD, D_MODEL), lambda i: (row_of_tile(i), 0, 0)),
                  _resident((1, D_MODEL)), _resident((D_MODEL, N_IN)), _resident((D_C, D_C)),
                  _resident((1, D_C)), _resident((1, D_C))],
        out_specs=out_specs,
        out_shape=out_shape,
        compiler_params=_params(1),
        name="proj",
    )(x, mods, g, w_in, bd, qg, kg)


def _depthwise_taps(pad, t, first, w_ref, n_taps):
    acc = None
    for r in range(SUBLANES):
        part = None
        for j in range(n_taps):
            if (first + j) % SUBLANES == r:
                term = pad[pl.ds(first + j - r, t + SUBLANES), :] * w_ref[j:j + 1, :]
                part = term if part is None else part + term
        if part is not None:
            part = part[r:r + t]
            acc = part if acc is None else acc + part
    return acc


def _conv_branches(t, apad, cpad, bg, caw_ref, cab_ref, lng_ref, lnb_ref, cbw_ref, wa_ref, wb_ref):
    ha = _depthwise_taps(apad, t, HALO - CONV_A // 2, caw_ref, CONV_A) + cab_ref[...]
    mu = jnp.mean(ha, axis=-1, keepdims=True)
    hc = ha - mu
    ln = hc * lax.rsqrt(jnp.mean(hc * hc, axis=-1, keepdims=True) + EPS) * lng_ref[...] + lnb_ref[...]
    ya = _dot((ln * jax.nn.sigmoid(ln)).astype(BF16), wa_ref[...])
    hb = _depthwise_taps(cpad, t, HALO - CONV_B // 2, cbw_ref, CONV_B)
    yb =_dot((bg.astype(F32) * hb).astype(BF16), wb_ref[...])
    return ya, yb


def _merge(x_ref, gate_ref, g2, ya, yb, yc, wm_ref, o_ref):
    m = (gate_ref[:, 0:D_MODEL].astype(F32) * ya
         + gate_ref[:, D_MODEL:2 * D_MODEL].astype(F32) * yb
         + gate_ref[:, 2 * D_MODEL:3 * D_MODEL].astype(F32) * yc)
    o_ref[...] = x_ref[...] + g2 * _dot(m.astype(BF16), wm_ref[...])


def _head_lane_masks():
    lane = lax.broadcasted_iota(jnp.int32, (1, LANES), 1)
    return [(lane >= e * HEAD_DIM) & (lane < (e + 1) * HEAD_DIM) for e in range(HEADS_PER_LANE_TILE)]


def _mix_ctx_kernel(x_ref, m_ref, a_ref, bg_ref, ch_ref, q_ref, k_ref, v_ref, gate_ref,
                    caw_ref, cab_ref, lng_ref, lnb_ref, cbw_ref, wa_ref, wb_ref, wc_ref, wm_ref,
                    o_ref, apad, cpad):
    t = x_ref.shape[0]
    zero_halo = jnp.zeros((HALO, D_A), F32)
    for pad, src in ((apad, a_ref), (cpad, ch_ref)):
        pad[0:HALO, :] = zero_halo
        pad[HALO:HALO + t, :] = src[...].astype(F32)
        pad[HALO + t:HALO + t + HALO, :] = zero_halo
    ya, yb = _conv_branches(t, apad, cpad, bg_ref[...], caw_ref, cab_ref, lng_ref, lnb_ref,
                            cbw_ref, wa_ref, wb_ref)
    masks = _head_lane_masks()
    o_tiles = []
    for mt in range(D_C // LANES):
        lanes = slice(mt * LANES, (mt + 1) * LANES)
        q2, k2, v2 = q_ref[:, lanes], k_ref[:, lanes], v_ref[:, lanes]
        o2 = None
        for e in range(HEADS_PER_LANE_TILE):
            s = _dot_t(jnp.where(masks[e], q2, jnp.zeros_like(q2)), k2)
            p = jnp.exp(s - jnp.max(s, axis=-1, keepdims=True))
            p = p * (1.0 / jnp.sum(p, axis=-1, keepdims=True))
            oe = _dot(p.astype(BF16), v2)
            o2 = oe if o2 is None else jnp.where(masks[e], oe, o2)
        o_tiles.append(o2)
    yc = _dot(jnp.concatenate(o_tiles, axis=1).astype(BF16), wc_ref[...])
    _merge(x_ref, gate_ref, m_ref[5:6, :], ya, yb, yc, wm_ref, o_ref)


def _mix_lat_kernel(x_ref, m_ref, a_ref, ap_ref, an_ref, bg_ref, ch_ref, cp_ref, cn_ref,
                    q_ref, k_ref, v_ref, kc_ref, vc_ref, bias_ref, gate_ref,
                    caw_ref, cab_ref, lng_ref, lnb_ref, cbw_ref, wa_ref, wb_ref, wc_ref, wm_ref,
                    o_ref, apad, cpad):
    t = x_ref.shape[0]
    i = pl.program_id(1)
    n_tiles = pl.num_programs(1)
    rows = n_tiles * ROWS_PER_TILE
    has_prev = (i > 0).astype(F32)
    has_next = (i < n_tiles - 1).astype(F32)
    for pad, prev, cur, nxt in ((apad, ap_ref, a_ref, an_ref), (cpad, cp_ref, ch_ref, cn_ref)):
        pad[0:HALO, :] = prev[...].astype(F32) * has_prev
        pad[HALO:HALO + t, :] = cur[...].astype(F32)
        pad[HALO + t:HALO + t + HALO, :] = nxt[...].astype(F32) * has_next
    ya, yb = _conv_branches(t, apad, cpad, bg_ref[...], caw_ref, cab_ref, lng_ref, lnb_ref,
                            cbw_ref, wa_ref, wb_ref)

    masks = _head_lane_masks()
    n_win = WIN_ROWS * GRID_W
    win_start, win_delta = [], []
    for rr in range(ROWS_PER_TILE):
        r = i * ROWS_PER_TILE + rr
        rs = jnp.clip(r - WIN_ROWS // 2, 0, rows - WIN_ROWS)
        win_start.append(pl.multiple_of(rs * GRID_W, GRID_W))
        win_delta.append(r - rs)
    q_rows = [slice(rr * GRID_W, (rr + 1) * GRID_W) for rr in range(ROWS_PER_TILE)]
    o_tiles = []
    for mt in range(D_C // LANES):
        lanes = slice(mt * LANES, (mt + 1) * LANES)
        q2 = q_ref[:, lanes]
        kc2 = kc_ref[:, lanes].astype(BF16)
        vc2 = vc_ref[:, lanes].astype(BF16)
        o2 = None
        for e in range(HEADS_PER_LANE_TILE):
            h = mt * HEADS_PER_LANE_TILE + e
            qm = jnp.where(masks[e], q2, jnp.zeros_like(q2))
            s_ctx = _dot_t(qm, kc2)
            s_loc = jnp.concatenate(
                [_dot_t(qm[q_rows[rr]], k_ref[pl.ds(win_start[rr], n_win), lanes]) + bias_ref[h, win_delta[rr]]
                 for rr in range(ROWS_PER_TILE)], axis=0)
            mx = jnp.maximum(jnp.max(s_loc, axis=-1, keepdims=True), jnp.max(s_ctx, axis=-1, keepdims=True))
            p_l = jnp.exp(s_loc - mx)
            p_c = jnp.exp(s_ctx - mx)
            inv = 1.0 / (jnp.sum(p_l, axis=-1, keepdims=True) + jnp.sum(p_c, axis=-1, keepdims=True))
            p_l = (p_l * inv).astype(BF16)
            p_c = (p_c * inv).astype(BF16)
            oe = jnp.concatenate(
                [_dot(p_l[q_rows[rr]], v_ref[pl.ds(win_start[rr], n_win), lanes])
                 for rr in range(ROWS_PER_TILE)], axis=0) + _dot(p_c, vc2)
            o2 = oe if o2 is None else jnp.where(masks[e], oe, o2)
        o_tiles.append(o2)
    yc = _dot(jnp.concatenate(o_tiles, axis=1).astype(BF16), wc_ref[...])
    _merge(x_ref, gate_ref, m_ref[5:6, :], ya, yb, yc, wm_ref, o_ref)


def _mixer_weight_specs():
    return [_resident((CONV_A, D_A)), _resident((1, D_A)), _resident((1, D_A)), _resident((1, D_A)),
            _resident((CONV_B, D_B)), _resident((D_A, D_MODEL)), _resident((D_B, D_MODEL)),
            _resident((D_C, D_MODEL)), _resident((D_MODEL, D_MODEL))]


def _mix_ctx(x, mods, pieces, weights, seq):
    n = x.shape[0]
    a, bg, ch, q, k, v, gates = pieces
    tok = lambda width: pl.BlockSpec((seq, width), lambda i: (i, 0))
    return pl.pallas_call(
        _mix_ctx_kernel,
        grid=(n // seq,),
        in_specs=[tok(D_MODEL), pl.BlockSpec((None, N_MOD, D_MODEL), lambda i: (0, 0, 0)),
                  tok(D_A), tok(D_B), tok(D_B), tok(D_C), tok(D_C), tok(D_C), tok(3 * D_MODEL)]
                 + _mixer_weight_specs(),
        out_specs=tok(D_MODEL),
        out_shape=jax.ShapeDtypeStruct((n, D_MODEL), F32),
        scratch_shapes=[pltpu.VMEM((seq + 2 * HALO, D_A), F32), pltpu.VMEM((seq + 2 * HALO, D_B), F32)],
        compiler_params=_params(1),
        name="mix_ctx",
    )(x, mods, a, bg, ch, q, k, v, gates, *weights)


def _mix_lat(x, mods, pieces, cache_k, cache_v, layer, bias, weights, batch, seq):
    a, bg, ch, q, k, v, gates = (p.reshape(batch, seq, p.shape[-1]) for p in pieces)
    t = ROWS_PER_TILE * GRID_W
    halo_per_tile = t // HALO
    n_halo = seq // HALO
    past = cache_k.shape[2]
    tok = lambda width: pl.BlockSpec((None, t, width), lambda b, i: (b, i, 0))
    prev = lambda width: pl.BlockSpec((None, HALO, width),
                                      lambda b, i: (b, jnp.maximum(i * halo_per_tile - 1, 0), 0))
    nxt = lambda width: pl.BlockSpec((None, HALO, width),
                                     lambda b, i: (b, jnp.minimum((i + 1) * halo_per_tile, n_halo - 1), 0))
    seq_resident = pl.BlockSpec((None, seq, D_C), lambda b, i: (b, 0, 0), pipeline_mode=pl.Buffered(1))
    cache = pl.BlockSpec((None, None, past, D_C), lambda b, i: (b, layer, 0, 0))
    out = pl.pallas_call(
        _mix_lat_kernel,
        grid=(batch, seq // t),
        in_specs=[tok(D_MODEL), pl.BlockSpec((None, N_MOD, D_MODEL), lambda b, i: (1 + b, 0, 0)),
                  tok(D_A), prev(D_A), nxt(D_A), tok(D_B), tok(D_B), prev(D_B), nxt(D_B),
                  tok(D_C), seq_resident, seq_resident, cache, cache,
                  _resident((N_HEADS, WIN_ROWS, GRID_W, WIN_ROWS * GRID_W)), tok(3 * D_MODEL)]
                 + _mixer_weight_specs(),
        out_specs=tok(D_MODEL),
        out_shape=jax.ShapeDtypeStruct((batch, seq, D_MODEL), F32),
        scratch_shapes=[pltpu.VMEM((t + 2 * HALO, D_A), F32), pltpu.VMEM((t + 2 * HALO, D_B), F32)],
        compiler_params=_params(2),
        name="mix_lat",
    )(x.reshape(batch, seq, D_MODEL), mods, a, a, a, bg, ch, ch, ch, q, k, v,
      cache_k, cache_v, bias, gates, *weights)
    return out.reshape(batch * seq, D_MODEL)


def _local_bias_table(rpb):
    qc = np.arange(GRID_W)[:, None]
    kc = np.arange(GRID_W)[None, :]
    win_start = np.clip(qc - WIN_COLS // 2, 0, GRID_W - WIN_COLS)
    valid = (kc >= win_start) & (kc < win_start + WIN_COLS)
    assert np.all(np.abs(kc - qc)[valid] < WIN_COLS)
    padded = jnp.pad(rpb, ((0, 0), (0, 0), (GRID_W - 1, GRID_W - 1)))
    off = GRID_W - 1 + WIN_COLS - 1
    col_bias = jnp.stack([padded[:, :, off - c:off - c + GRID_W] for c in range(GRID_W)], axis=1)
    col_bias = jnp.where(jnp.asarray(valid)[None, :, None, :], col_bias, NEG_INF)
    tab = jnp.stack([col_bias[:, :, WIN_ROWS - 1 - d:2 * WIN_ROWS - 1 - d] for d in range(WIN_ROWS)], axis=1)
    return tab.reshape(N_HEADS, WIN_ROWS, GRID_W, WIN_ROWS * GRID_W)


def kernel(x_prompt, x_sample, cache_k, cache_v, c, c_ctx, w_ada, b_ada, g_ff1, w_ff1_gate, w_ff1_up,
           w_ff1_down, g_mix, w_in, conv_a_w, conv_a_b, ln_a_g, ln_a_b, w_a_out, conv_b_w, w_b_out,
           q_norm_g, k_norm_g, rpb, w_c_out, w_merge, g_ff2, w_ff2_gate, w_ff2_up, w_ff2_down):
    batch, seq, _ = x_prompt.shape
    dec_batch, dec_seq, _ = x_sample.shape
    past = cache_k.shape[2]
    assert dec_batch + 1 <= MOD_ROWS and dec_seq % (ROWS_PER_TILE * GRID_W) == 0
    assert (batch * seq) % TM == 0 and dec_seq % TM == 0 and seq % HALO == 0

    cvecs = jnp.zeros((MOD_ROWS, D_MODEL), F32).at[0].set(c_ctx).at[1:1 + dec_batch].set(c)
    mods = _modulation(cvecs, w_ada, b_ada)
    ctx_row = lambda i: 0
    lat_row = lambda i: 1 + (i * TM) // dec_seq

    head = np.arange(D_C) // HEAD_DIM
    bd = jnp.asarray((head[:, None] == head[None, :]) / HEAD_DIM, BF16)
    ck = cache_k.reshape(dec_batch, DEPTH, past, D_C)
    cv = cache_v.reshape(dec_batch, DEPTH, past, D_C)

    h_ctx = x_prompt.reshape(batch * seq, D_MODEL)
    h_lat = x_sample.reshape(dec_batch * dec_seq, D_MODEL)
    new_ks, new_vs = [], []
    row = lambda v: v.reshape(1, -1)
    for l in range(DEPTH):
        ff1 = (row(g_ff1[l]), w_ff1_gate[l].astype(BF16), w_ff1_up[l].astype(BF16), w_ff1_down[l].astype(BF16))
        ff2 = (row(g_ff2[l]), w_ff2_gate[l].astype(BF16), w_ff2_up[l].astype(BF16), w_ff2_down[l].astype(BF16))
        qg = row(jnp.tile(q_norm_g[l], N_HEADS) * HEAD_DIM ** -0.5)
        kg = row(jnp.tile(k_norm_g[l], N_HEADS))
        proj_w = (row(g_mix[l]), w_in[l].astype(BF16), bd, qg, kg)
        mix_w = (conv_a_w[l], row(conv_a_b[l]), row(ln_a_g[l]), row(ln_a_b[l]), conv_b_w[l],
                 w_a_out[l].astype(BF16), w_b_out[l].astype(BF16), w_c_out[l].astype(BF16),
                 w_merge[l].astype(BF16))
        bias = _local_bias_table(rpb[l])

        h_ctx = _ffn(h_ctx, mods[l], 0, ctx_row, *ff1)
        *pieces, k_new, v_new = _proj(h_ctx, mods[l], ctx_row, *proj_w, emit_kv=True)
        new_ks.append(k_new.reshape(batch, seq, N_HEADS, HEAD_DIM))
        new_vs.append(v_new.reshape(batch, seq, N_HEADS, HEAD_DIM))
        h_ctx = _mix_ctx(h_ctx, mods[l], pieces, mix_w, seq)
        h_ctx = _ffn(h_ctx, mods[l], 6, ctx_row, *ff2)

        h_lat = _ffn(h_lat, mods[l], 0, lat_row, *ff1)
        pieces = _proj(h_lat, mods[l], lat_row, *proj_w, emit_kv=False)
        h_lat = _mix_lat(h_lat, mods[l], pieces, ck, cv, l, bias, mix_w, dec_batch, dec_seq)
        h_lat = _ffn(h_lat, mods[l], 6, lat_row, *ff2)

    return (h_ctx.reshape(batch, seq, D_MODEL), h_lat.reshape(dec_batch, dec_seq, D_MODEL),
            jnp.stack(new_ks, axis=1), jnp.stack(new_vs, axis=1))
```

```python
from functools import partial

import numpy as np
import jax
import jax.numpy as jnp
from jax import lax
from jax.experimental import pallas as pl
from jax.experimental.pallas import tpu as pltpu

D_MODEL = 1024
DEPTH = 2
GRID_W = 64
D_A = 512
CONV_A = 31
D_B = 512
CONV_B = 3
N_HEADS = 8
HEAD_DIM = 64
D_C = N_HEADS * HEAD_DIM
WIN_ROWS = 8
WIN_COLS = 16
D_FF = 2816
N_MOD = 9
N_IN = 2 * D_A + 3 * D_B + 3 * D_C + 3 * D_MODEL
EPS = 1e-6
NEG_INF = -1e30
F32_MIN = float(np.finfo(np.float32).min)
LOG2E = 1.4426950408889634

LANES = 128
SUBLANES = 8
HEADS_PER_LANE_TILE = LANES // HEAD_DIM
N_BIAS_ROWS = 2 * WIN_ROWS - 1
ROWS_PER_CHUNK = LANES // GRID_W
N_BIAS_CHUNKS = (N_BIAS_ROWS - ROWS_PER_CHUNK) // ROWS_PER_CHUNK + 1
HALO = 16
MOD_ROWS = 16
VMEM_LIMIT = 60 * 1024 * 1024
TM = 512
ROWS_PER_TILE = 8

F32 = jnp.float32
BF16 = jnp.bfloat16


def _dot(a, b):
    return jnp.dot(a, b, preferred_element_type=F32)


def _dot_t(a, b):
    return lax.dot_general(a, b, (((1,), (1,)), ((), ())), preferred_element_type=F32)


def _resident(shape):
    zeros = (0,) * len(shape)
    return pl.BlockSpec(shape, lambda *_: zeros, pipeline_mode=pl.Buffered(1))


def _params(n_grid_dims):
    return pltpu.CompilerParams(dimension_semantics=("arbitrary",) * n_grid_dims,
                                vmem_limit_bytes=VMEM_LIMIT)


def _ada_ln(x, g, shift, scale):
    y = x * lax.rsqrt(jnp.mean(x * x, axis=-1, keepdims=True) + EPS)
    return (y * g) * (1.0 + scale) + shift


def _mods_kernel(c_ref, w_ref, b_ref, o_ref):
    c = c_ref[...]
    s = (c * jax.nn.sigmoid(c)).astype(BF16)
    o_ref[...] = _dot(s, w_ref[...].astype(BF16)) + b_ref[...]


def _modulation(cvecs, w_ada, b_ada):
    out = pl.pallas_call(
        _mods_kernel,
        grid=(DEPTH, N_MOD),
        in_specs=[pl.BlockSpec((MOD_ROWS, D_MODEL), lambda l, j: (0, 0)),
                  pl.BlockSpec((None, D_MODEL, D_MODEL), lambda l, j: (l, 0, j)),
                  pl.BlockSpec((None, 1, D_MODEL), lambda l, j: (l, 0, j))],
        out_specs=pl.BlockSpec((None, MOD_ROWS, D_MODEL), lambda l, j: (l, 0, j)),
        out_shape=jax.ShapeDtypeStruct((DEPTH, MOD_ROWS, N_MOD * D_MODEL), F32),
        compiler_params=_params(2),
        name="modulation",
    )(cvecs, w_ada, b_ada.reshape(DEPTH, 1, N_MOD * D_MODEL))
    return out.reshape(DEPTH, MOD_ROWS, N_MOD, D_MODEL)


def _ffn_kernel(mod0, x_ref, m_ref, g_ref, wg_ref, wu_ref, wd_ref, o_ref):
    x = x_ref[...]
    shift, scale, gate = (m_ref[mod0 + i:mod0 + i + 1, :] for i in range(3))
    u = _ada_ln(x, g_ref[...], shift, scale).astype(BF16)
    a = _dot(u, wg_ref[...])
    b = _dot(u, wu_ref[...])
    h = (a * jax.nn.sigmoid(a) * b).astype(BF16)
    o_ref[...] = x + (0.5 * gate) * _dot(h, wd_ref[...])


def _ffn(x, mods, mod0, row_of_tile, g, wg, wu, wd):
    n = x.shape[0]
    return pl.pallas_call(
        partial(_ffn_kernel, mod0),
        grid=(n // TM,),
        in_specs=[pl.BlockSpec((TM, D_MODEL), lambda i: (i, 0)),
                  pl.BlockSpec((None, N_MOD, D_MODEL), lambda i: (row_of_tile(i), 0, 0)),
                  _resident((1, D_MODEL)),
                  _resident((D_MODEL, D_FF)), _resident((D_MODEL, D_FF)), _resident((D_FF, D_MODEL))],
        out_specs=pl.BlockSpec((TM, D_MODEL), lambda i: (i, 0)),
        out_shape=jax.ShapeDtypeStruct((n, D_MODEL), F32),
        compiler_params=_params(1),
        name="ffn",
    )(x, mods, g, wg, wu, wd)


def _head_rms(t, bd, g):
    ms = _dot((t * t).astype(BF16), bd)
    return t * lax.rsqrt(ms + EPS) * g


def _proj_kernel(emit_kv, x_ref, m_ref, g_ref, w_ref, bd_ref, qg_ref, kg_ref, *out_refs):
    a_ref, bg_ref, ch_ref, q_ref, k_ref, v_ref, gate_ref = out_refs[:7]
    shift, scale = m_ref[3:4, :], m_ref[4:5, :]
    u = _ada_ln(x_ref[...], g_ref[...], shift, scale).astype(BF16)

    def seg(start, width):
        return _dot(u, w_ref[:, start:start + width])

    o = 0
    a_ref[...] = (seg(o, D_A) * jax.nn.sigmoid(seg(o + D_A, D_A))).astype(BF16)
    o += 2 * D_A
    bg_ref[...] = seg(o, D_B).astype(BF16)
    ch_ref[...] = (seg(o + D_B, D_B) * seg(o + 2 * D_B, D_B)).astype(BF16)
    o += 3 * D_B
    bd = bd_ref[...]
    q_ref[...] = _head_rms(seg(o, D_C), bd, qg_ref[...]).astype(BF16)
    k = _head_rms(seg(o + D_C, D_C), bd, kg_ref[...])
    v = seg(o + 2 * D_C, D_C)
    k_ref[...] = k.astype(BF16)
    v_ref[...] = v.astype(BF16)
    if emit_kv:
        out_refs[7][...] = k
        out_refs[8][...] = v
    o += 3 * D_C
    for j in range(3):
        gate_ref[:, j * D_MODEL:(j + 1) * D_MODEL] = jax.nn.sigmoid(seg(o + j * D_MODEL, D_MODEL)).astype(BF16)


def _proj(x, mods, row_of_tile, g, w_in, bd, qg, kg, emit_kv):
    n = x.shape[0]
    tok = lambda width: pl.BlockSpec((TM, width), lambda i: (i, 0))
    widths = [D_A, D_B, D_B, D_C, D_C, D_C, 3 * D_MODEL]
    out_shape = [jax.ShapeDtypeStruct((n, w), BF16) for w in widths]
    out_specs = [tok(w) for w in widths]
    if emit_kv:
        out_shape += [jax.ShapeDtypeStruct((n, D_C), F32)] * 2
        out_specs += [tok(D_C)] * 2
    return pl.pallas_call(
        partial(_proj_kernel, emit_kv),
        grid=(n // TM,),
        in_specs=[tok(D_MODEL),
                  pl.BlockSpec((None, N_MOD, D_MODEL), lambda i: (row_of_tile(i), 0, 0)),
                  _resident((1, D_MODEL)), _resident((D_MODEL, N_IN)), _resident((D_C, D_C)),
                  _resident((1, D_C)), _resident((1, D_C))],
        out_specs=out_specs,
        out_shape=out_shape,
        compiler_params=_params(1),
        name="proj",
    )(x, mods, g, w_in, bd, qg, kg)


def _depthwise_taps(pad, rows, lanes, first, w_ref, n_taps):
    acc = None
    for r in range(SUBLANES):
        part = None
        for j in range(n_taps):
            if (first + j) % SUBLANES == r:
                src = pl.ds(rows.start + first + j - r, rows.stop - rows.start + SUBLANES)
                term = pad[src, lanes] * w_ref[j:j + 1, lanes]
                part = term if part is None else part + term
        if part is not None:
            part = part[r:r + rows.stop - rows.start]
            acc = part if acc is None else acc + part
    return acc


def _conv_lane_tile(t, lanes, apad, cpad, caw_ref, cab_ref, cbw_ref, ha_buf, hb_buf):
    block = t // HEADS_PER_LANE_TILE
    anchors = []
    for start in range(0, t, block):
        rows = slice(start, start + block)
        ha = _depthwise_taps(apad, rows, lanes, HALO - CONV_A // 2, caw_ref, CONV_A) + cab_ref[:, lanes]
        hb = _depthwise_taps(cpad, rows, lanes, HALO - CONV_B // 2, cbw_ref, CONV_B)
        ha_buf[rows, lanes] = ha
        hb_buf[rows, lanes] = hb
        anchors.append(jnp.minimum(jnp.min(jnp.minimum(ha, hb), axis=0, keepdims=True), F32_MIN))
    return anchors


def _conv_outputs(ha, hb, bg, lng_ref, lnb_ref, wa_ref, wb_ref):
    mu = jnp.mean(ha, axis=-1, keepdims=True)
    hc = ha - mu
    ln = hc * lax.rsqrt(jnp.mean(hc * hc, axis=-1, keepdims=True) + EPS) * lng_ref[...] + lnb_ref[...]
    ya = _dot((ln * jax.nn.sigmoid(ln)).astype(BF16), wa_ref[...])
    yb = _dot((bg.astype(F32) * hb).astype(BF16), wb_ref[...])
    return ya, yb


def _merge(x_ref, gate_ref, g2, ya, yb, yc, wm_ref, o_ref):
    m = (gate_ref[:, 0:D_MODEL].astype(F32) * ya
         + gate_ref[:, D_MODEL:2 * D_MODEL].astype(F32) * yb
         + gate_ref[:, 2 * D_MODEL:3 * D_MODEL].astype(F32) * yc)
    o_ref[...] = x_ref[...] + g2 * _dot(m.astype(BF16), wm_ref[...])


def _head_lane_masks():
    lane = lax.broadcasted_iota(jnp.int32, (1, LANES), 1)
    return [(lane >= e * HEAD_DIM) & (lane < (e + 1) * HEAD_DIM) for e in range(HEADS_PER_LANE_TILE)]


def _mixer_scratch(t):
    return [pltpu.VMEM((t + 2 * HALO, D_A), F32), pltpu.VMEM((t + 2 * HALO, D_B), F32),
            pltpu.VMEM((t, D_A), F32), pltpu.VMEM((t, D_B), F32), pltpu.VMEM((t, D_C), BF16)]


def _mix_ctx_kernel(x_ref, m_ref, a_ref, bg_ref, ch_ref, q_ref, k_ref, v_ref, gate_ref,
                    caw_ref, cab_ref, lng_ref, lnb_ref, cbw_ref, wa_ref, wb_ref, wc_ref, wm_ref,
                    o_ref, apad, cpad, ha_buf, hb_buf, o_buf):
    t = x_ref.shape[0]
    zero_halo = jnp.zeros((HALO, D_A), F32)
    for pad, src in ((apad, a_ref), (cpad, ch_ref)):
        pad[0:HALO, :] = zero_halo
        pad[HALO:HALO + t, :] = src[...].astype(F32)
        pad[HALO + t:HALO + t + HALO, :] = zero_halo
    masks = _head_lane_masks()

    for mt in range(D_C // LANES):
        lanes = slice(mt * LANES, (mt + 1) * LANES)
        _conv_lane_tile(t, lanes, apad, cpad, caw_ref, cab_ref, cbw_ref, ha_buf, hb_buf)
        q2, k2, v2 = q_ref[:, lanes], k_ref[:, lanes], v_ref[:, lanes]
        o2 = None
        for e in range(HEADS_PER_LANE_TILE):
            s = _dot_t(jnp.where(masks[e], q2, jnp.zeros_like(q2)), k2)
            p = jnp.exp2(s - jnp.max(s, axis=-1, keepdims=True))
            inv = 1.0 / jnp.sum(p, axis=-1, keepdims=True)
            oe = _dot(p.astype(BF16), v2) * inv
            o2 = oe if o2 is None else jnp.where(masks[e], oe, o2)
        o_buf[:, lanes] = o2.astype(BF16)
    ya, yb = _conv_outputs(ha_buf[...], hb_buf[...], bg_ref[...], lng_ref, lnb_ref, wa_ref, wb_ref)
    yc = _dot(o_buf[...], wc_ref[...])
    _merge(x_ref, gate_ref, m_ref[5:6, :], ya, yb, yc, wm_ref, o_ref)


def _mix_lat_kernel(x_ref, m_ref, a_ref, ap_ref, an_ref, bg_ref, ch_ref, cp_ref, cn_ref,
                    q_ref, k_ref, v_ref, kc_ref, vc_ref, bias_ref, gate_ref,
                    caw_ref, cab_ref, lng_ref, lnb_ref, cbw_ref, wa_ref, wb_ref, wc_ref, wm_ref,
                    o_ref, apad, cpad, ha_buf, hb_buf, o_buf):
    t = x_ref.shape[0]
    i = pl.program_id(1)
    n_tiles = pl.num_programs(1)
    rows = n_tiles * ROWS_PER_TILE
    has_prev = (i > 0).astype(F32)
    has_next = (i < n_tiles - 1).astype(F32)
    for pad, prev, cur, nxt in ((apad, ap_ref, a_ref, an_ref), (cpad, cp_ref, ch_ref, cn_ref)):
        pad[0:HALO, :] = prev[...].astype(F32) * has_prev
        pad[HALO:HALO + t, :] = cur[...].astype(F32)
        pad[HALO + t:HALO + t + HALO, :] = nxt[...].astype(F32) * has_next
    masks = _head_lane_masks()
    n_win = WIN_ROWS * GRID_W
    win_start, bias_parity, bias_chunk = [], [], []
    for rr in range(ROWS_PER_TILE):
        r = i * ROWS_PER_TILE + rr
        rs = jnp.clip(r - WIN_ROWS // 2, 0, rows - WIN_ROWS)
        win_start.append(pl.multiple_of(rs * GRID_W, GRID_W))
        first_rel_row = WIN_ROWS - 1 - (r - rs)
        bias_parity.append(lax.rem(first_rel_row, ROWS_PER_CHUNK))
        bias_chunk.append(lax.div(first_rel_row, ROWS_PER_CHUNK))

    def local_bias(h, rr, anchor):
        chunks = [bias_ref[h, bias_parity[rr], bias_chunk[rr] + u] for u in range(n_win // LANES)]
        if rr == 0:
            chunks[0] = jnp.maximum(chunks[0], anchor)
        return jnp.concatenate(chunks, axis=1)
    q_rows = [slice(rr * GRID_W, (rr + 1) * GRID_W) for rr in range(ROWS_PER_TILE)]
    for mt in range(D_C // LANES):
        lanes = slice(mt * LANES, (mt + 1) * LANES)
        anchors = _conv_lane_tile(t, lanes, apad, cpad, caw_ref, cab_ref, cbw_ref, ha_buf, hb_buf)
        q2 = q_ref[:, lanes]
        kc2 = kc_ref[:, lanes].astype(BF16)
        vc2 = vc_ref[:, lanes].astype(BF16)
        o2 = None
        for e in range(HEADS_PER_LANE_TILE):
            h = mt * HEADS_PER_LANE_TILE + e
            qm = jnp.where(masks[e], q2, jnp.zeros_like(q2))
            s_ctx = _dot_t(qm, kc2)
            s_loc = jnp.concatenate(
                [_dot_t(qm[q_rows[rr]], k_ref[pl.ds(win_start[rr], n_win), lanes]) + local_bias(h, rr, anchors[e])
                 for rr in range(ROWS_PER_TILE)], axis=0)
            mx = jnp.maximum(jnp.max(s_loc, axis=-1, keepdims=True), jnp.max(s_ctx, axis=-1, keepdims=True))
            p_l = jnp.exp2(s_loc - mx)
            p_c = jnp.exp2(s_ctx - mx)
            inv = 1.0 / (jnp.sum(p_l, axis=-1, keepdims=True) + jnp.sum(p_c, axis=-1, keepdims=True))
            p_l = p_l.astype(BF16)
            oe = jnp.concatenate(
                [_dot(p_l[q_rows[rr]], v_ref[pl.ds(win_start[rr], n_win), lanes])
                 for rr in range(ROWS_PER_TILE)], axis=0) + _dot(p_c.astype(BF16), vc2)
            oe = oe * inv
            o2 = oe if o2 is None else jnp.where(masks[e], oe, o2)
        o_buf[:, lanes] = o2.astype(BF16)
    ya, yb = _conv_outputs(ha_buf[...], hb_buf[...], bg_ref[...], lng_ref, lnb_ref, wa_ref, wb_ref)
    yc = _dot(o_buf[...], wc_ref[...])
    _merge(x_ref, gate_ref, m_ref[5:6, :], ya, yb, yc, wm_ref, o_ref)


def _mixer_weight_specs():
    return [_resident((CONV_A, D_A)), _resident((1, D_A)), _resident((1, D_A)), _resident((1, D_A)),
            _resident((CONV_B, D_B)), _resident((D_A, D_MODEL)), _resident((D_B, D_MODEL)),
            _resident((D_C, D_MODEL)), _resident((D_MODEL, D_MODEL))]


def _mix_ctx(x, mods, pieces, weights, seq):
    n = x.shape[0]
    a, bg, ch, q, k, v, gates = pieces
    tok = lambda width: pl.BlockSpec((seq, width), lambda i: (i, 0))
    return pl.pallas_call(
        _mix_ctx_kernel,
        grid=(n // seq,),
        in_specs=[tok(D_MODEL), pl.BlockSpec((None, N_MOD, D_MODEL), lambda i: (0, 0, 0)),
                  tok(D_A), tok(D_B), tok(D_B), tok(D_C), tok(D_C), tok(D_C), tok(3 * D_MODEL)]
                 + _mixer_weight_specs(),
        out_specs=tok(D_MODEL),
        out_shape=jax.ShapeDtypeStruct((n, D_MODEL), F32),
        scratch_shapes=_mixer_scratch(seq),
        compiler_params=_params(1),
        name="mix_ctx",
    )(x, mods, a, bg, ch, q, k, v, gates, *weights)


def _mix_lat(x, mods, pieces, cache_k, cache_v, layer, bias, weights, batch, seq):
    a, bg, ch, q, k, v, gates = (p.reshape(batch, seq, p.shape[-1]) for p in pieces)
    t = ROWS_PER_TILE * GRID_W
    halo_per_tile = t // HALO
    n_halo = seq // HALO
    past = cache_k.shape[2]
    tok = lambda width: pl.BlockSpec((None, t, width), lambda b, i: (b, i, 0))
    prev = lambda width: pl.BlockSpec((None, HALO, width),
                                      lambda b, i: (b, jnp.maximum(i * halo_per_tile - 1, 0), 0))
    nxt = lambda width: pl.BlockSpec((None, HALO, width),
                                     lambda b, i: (b, jnp.minimum((i + 1) * halo_per_tile, n_halo - 1), 0))
    seq_resident = pl.BlockSpec((None, seq, D_C), lambda b, i: (b, 0, 0), pipeline_mode=pl.Buffered(1))
    cache = pl.BlockSpec((None, None, past, D_C), lambda b, i: (b, layer, 0, 0), pipeline_mode=pl.Buffered(1))
    out = pl.pallas_call(
        _mix_lat_kernel,
        grid=(batch, seq // t),
        in_specs=[tok(D_MODEL), pl.BlockSpec((None, N_MOD, D_MODEL), lambda b, i: (1 + b, 0, 0)),
                  tok(D_A), prev(D_A), nxt(D_A), tok(D_B), tok(D_B), prev(D_B), nxt(D_B),
                  tok(D_C), seq_resident, seq_resident, cache, cache,
                  _resident((N_HEADS, ROWS_PER_CHUNK, N_BIAS_CHUNKS, GRID_W, LANES)), tok(3 * D_MODEL)]
                 + _mixer_weight_specs(),
        out_specs=tok(D_MODEL),
        out_shape=jax.ShapeDtypeStruct((batch, seq, D_MODEL), F32),
        scratch_shapes=_mixer_scratch(t),
        compiler_params=_params(2),
        name="mix_lat",
    )(x.reshape(batch, seq, D_MODEL), mods, a, a, a, bg, ch, ch, ch, q, k, v,
      cache_k, cache_v, bias, gates, *weights)
    return out.reshape(batch * seq, D_MODEL)


def _local_bias_table(rpb):
    qc = np.arange(GRID_W)[:, None]
    kc = np.arange(GRID_W)[None, :]
    win_start = np.clip(qc - WIN_COLS // 2, 0, GRID_W - WIN_COLS)
    valid = (kc >= win_start) & (kc < win_start + WIN_COLS)
    assert np.all(np.abs(kc - qc)[valid] < WIN_COLS)
    padded = jnp.pad(rpb, ((0, 0), (0, 0), (GRID_W - 1, GRID_W - 1)))
    off = GRID_W - 1 + WIN_COLS - 1
    col_bias = jnp.stack([padded[:, :, off - c:off - c + GRID_W] for c in range(GRID_W)], axis=1)
    col_bias = jnp.where(jnp.asarray(valid)[None, :, None, :], col_bias * LOG2E, NEG_INF)
    flat = col_bias.reshape(N_HEADS, GRID_W, N_BIAS_ROWS * GRID_W)
    return jnp.stack([jnp.stack([flat[:, :, (ROWS_PER_CHUNK * j + p) * GRID_W:][:, :, :LANES]
                                 for j in range(N_BIAS_CHUNKS)], axis=1)
                      for p in range(ROWS_PER_CHUNK)], axis=1)


def kernel(x_prompt, x_sample, cache_k, cache_v, c, c_ctx, w_ada, b_ada, g_ff1, w_ff1_gate, w_ff1_up,
           w_ff1_down, g_mix, w_in, conv_a_w, conv_a_b, ln_a_g, ln_a_b, w_a_out, conv_b_w, w_b_out,
           q_norm_g, k_norm_g, rpb, w_c_out, w_merge, g_ff2, w_ff2_gate, w_ff2_up, w_ff2_down):
    batch, seq, _ = x_prompt.shape
    dec_batch, dec_seq, _ = x_sample.shape
    past = cache_k.shape[2]
    assert dec_batch + 1 <= MOD_ROWS and dec_seq % (ROWS_PER_TILE * GRID_W) == 0
    assert (batch * seq) % TM == 0 and dec_seq % TM == 0 and seq % HALO == 0

    cvecs = jnp.zeros((MOD_ROWS, D_MODEL), F32).at[0].set(c_ctx).at[1:1 + dec_batch].set(c)
    mods = _modulation(cvecs, w_ada, b_ada)
    ctx_row = lambda i: 0
    lat_row = lambda i: 1 + (i * TM) // dec_seq

    head = np.arange(D_C) // HEAD_DIM
    bd = jnp.asarray((head[:, None] == head[None, :]) / HEAD_DIM, BF16)
    ck = cache_k.reshape(dec_batch, DEPTH, past, D_C)
    cv = cache_v.reshape(dec_batch, DEPTH, past, D_C)

    h_ctx = x_prompt.reshape(batch * seq, D_MODEL)
    h_lat = x_sample.reshape(dec_batch * dec_seq, D_MODEL)
    new_ks, new_vs = [], []
    row = lambda v: v.reshape(1, -1)
    for l in range(DEPTH):
        ff1 = (row(g_ff1[l]), w_ff1_gate[l].astype(BF16), w_ff1_up[l].astype(BF16), w_ff1_down[l].astype(BF16))
        ff2 = (row(g_ff2[l]), w_ff2_gate[l].astype(BF16), w_ff2_up[l].astype(BF16), w_ff2_down[l].astype(BF16))
        qg = row(jnp.tile(q_norm_g[l], N_HEADS) * (HEAD_DIM ** -0.5 * LOG2E))
        kg = row(jnp.tile(k_norm_g[l], N_HEADS))
        proj_w = (row(g_mix[l]), w_in[l].astype(BF16), bd, qg, kg)
        mix_w = (conv_a_w[l], row(conv_a_b[l]), row(ln_a_g[l]), row(ln_a_b[l]), conv_b_w[l],
                 w_a_out[l].astype(BF16), w_b_out[l].astype(BF16), w_c_out[l].astype(BF16),
                 w_merge[l].astype(BF16))
        bias = _local_bias_table(rpb[l])

        h_ctx = _ffn(h_ctx, mods[l], 0, ctx_row, *ff1)
        *pieces, k_new, v_new = _proj(h_ctx, mods[l], ctx_row, *proj_w, emit_kv=True)
        new_ks.append(k_new.reshape(batch, seq, N_HEADS, HEAD_DIM))
        new_vs.append(v_new.reshape(batch, seq, N_HEADS, HEAD_DIM))
        h_ctx = _mix_ctx(h_ctx, mods[l], pieces, mix_w, seq)
        h_ctx = _ffn(h_ctx, mods[l], 6, ctx_row, *ff2)

        h_lat = _ffn(h_lat, mods[l], 0, lat_row, *ff1)
        pieces = _proj(h_lat, mods[l], lat_row, *proj_w, emit_kv=False)
        h_lat = _mix_lat(h_lat, mods[l], pieces, ck, cv, l, bias, mix_w, dec_batch, dec_seq)
        h_lat = _ffn(h_lat, mods[l], 6, lat_row, *ff2)

    return (h_ctx.reshape(batch, seq, D_MODEL), h_lat.reshape(dec_batch, dec_seq, D_MODEL),
            jnp.stack(new_ks, axis=1), jnp.stack(new_vs, axis=1))
```

```python
from functools import partial

import numpy as np
import jax
import jax.numpy as jnp
from jax import lax
from jax.experimental import pallas as pl
from jax.experimental.pallas import tpu as pltpu

D_MODEL = 1024
DEPTH = 2
GRID_W = 64
D_A = 512
CONV_A = 31
D_B = 512
CONV_B = 3
N_HEADS = 8
HEAD_DIM = 64
D_C = N_HEADS * HEAD_DIM
WIN_ROWS = 8
WIN_COLS = 16
D_FF = 2816
N_MOD = 9
N_IN = 2 * D_A + 3 * D_B + 3 * D_C + 3 * D_MODEL
EPS = 1e-6
NEG_INF = -1e30
F32_MIN = float(np.finfo(np.float32).min)
LOG2E = 1.4426950408889634

LANES = 128
SUBLANES = 8
HEADS_PER_LANE_TILE = LANES // HEAD_DIM
N_BIAS_ROWS = 2 * WIN_ROWS - 1
ROWS_PER_CHUNK = LANES // GRID_W
N_BIAS_CHUNKS = (N_BIAS_ROWS - ROWS_PER_CHUNK) // ROWS_PER_CHUNK + 1
HALO = 16
CONV_BLOCKS = 8
MOD_ROWS = 16
VMEM_LIMIT = 60 * 1024 * 1024
TM = 512
ROWS_PER_TILE = 8

F32 = jnp.float32
BF16 = jnp.bfloat16


def _dot(a, b):
    return jnp.dot(a, b, preferred_element_type=F32)


def _dot_t(a, b):
    return lax.dot_general(a, b, (((1,), (1,)), ((), ())), preferred_element_type=F32)


def _resident(shape):
    zeros = (0,) * len(shape)
    return pl.BlockSpec(shape, lambda *_: zeros, pipeline_mode=pl.Buffered(1))


def _layer_resident(shape, layer):
    index = (layer,) + (0,) * len(shape)
    return pl.BlockSpec((None,) + tuple(shape), lambda *_: index, pipeline_mode=pl.Buffered(1))


def _mods_spec(layer, row_of_step):
    return pl.BlockSpec((None, None, N_MOD, D_MODEL), lambda *step: (layer, row_of_step(*step), 0, 0))


def _params(n_grid_dims):
    return pltpu.CompilerParams(dimension_semantics=("arbitrary",) * n_grid_dims,
                                vmem_limit_bytes=VMEM_LIMIT)


def _ada_ln(x, g, shift, scale):
    y = x * lax.rsqrt(jnp.mean(x * x, axis=-1, keepdims=True) + EPS)
    return (y * g) * (1.0 + scale) + shift


def _mods_kernel(c_ref, w_ref, b_ref, o_ref):
    c = c_ref[...]
    s = (c * jax.nn.sigmoid(c)).astype(BF16)
    o_ref[...] = _dot(s, w_ref[...].astype(BF16)) + b_ref[...]


def _modulation(cvecs, w_ada, b_ada):
    out = pl.pallas_call(
        _mods_kernel,
        grid=(DEPTH, N_MOD),
        in_specs=[pl.BlockSpec((MOD_ROWS, D_MODEL), lambda l, j: (0, 0)),
                  pl.BlockSpec((None, D_MODEL, D_MODEL), lambda l, j: (l, 0, j)),
                  pl.BlockSpec((None, 1, D_MODEL), lambda l, j: (l, 0, j))],
        out_specs=pl.BlockSpec((None, MOD_ROWS, D_MODEL), lambda l, j: (l, 0, j)),
        out_shape=jax.ShapeDtypeStruct((DEPTH, MOD_ROWS, N_MOD * D_MODEL), F32),
        compiler_params=_params(2),
        name="modulation",
    )(cvecs, w_ada, b_ada.reshape(DEPTH, 1, N_MOD * D_MODEL))
    return out.reshape(DEPTH, MOD_ROWS, N_MOD, D_MODEL)


def _ffn_kernel(mod0, x_ref, m_ref, g_ref, wg_ref, wu_ref, wd_ref, o_ref):
    x = x_ref[...]
    shift, scale, gate = (m_ref[mod0 + i:mod0 + i + 1, :] for i in range(3))
    u = _ada_ln(x, g_ref[...], shift, scale).astype(BF16)
    a = _dot(u, wg_ref[...])
    b = _dot(u, wu_ref[...])
    h = (a * jax.nn.sigmoid(a) * b).astype(BF16)
    o_ref[...] = x + (0.5 * gate) * _dot(h, wd_ref[...])


def _ffn(x, mods, layer, mod0, row_of_tile, g, wg, wu, wd):
    n = x.shape[0]
    return pl.pallas_call(
        partial(_ffn_kernel, mod0),
        grid=(n // TM,),
        in_specs=[pl.BlockSpec((TM, D_MODEL), lambda i: (i, 0)),
                  _mods_spec(layer, row_of_tile),
                  _layer_resident((1, D_MODEL), layer),
                  _layer_resident((D_MODEL, D_FF), layer), _layer_resident((D_MODEL, D_FF), layer),
                  _layer_resident((D_FF, D_MODEL), layer)],
        out_specs=pl.BlockSpec((TM, D_MODEL), lambda i: (i, 0)),
        out_shape=jax.ShapeDtypeStruct((n, D_MODEL), F32),
        compiler_params=_params(1),
        name="ffn",
    )(x, mods, g, wg, wu, wd)


def _head_rms(t, bd, g):
    ms = _dot((t * t).astype(BF16), bd)
    return t * lax.rsqrt(ms + EPS) * g


def _proj_kernel(n_alias, x_ref, m_ref, g_ref, w_ref, bd_ref, qg_ref, kg_ref, *refs):
    out_refs = refs[n_alias:]
    a_ref, bg_ref, ch_ref, q_ref, k_ref, v_ref, gate_ref = out_refs[:7]
    shift, scale = m_ref[3:4, :], m_ref[4:5, :]
    u = _ada_ln(x_ref[...], g_ref[...], shift, scale).astype(BF16)

    def seg(start, width):
        return _dot(u, w_ref[:, start:start + width])

    o = 0
    a_ref[...] = (seg(o, D_A) * jax.nn.sigmoid(seg(o + D_A, D_A))).astype(BF16)
    o += 2 * D_A
    bg_ref[...] = seg(o, D_B).astype(BF16)
    ch_ref[...] = (seg(o + D_B, D_B) * seg(o + 2 * D_B, D_B)).astype(BF16)
    o += 3 * D_B
    bd = bd_ref[...]
    q_ref[...] = _head_rms(seg(o, D_C), bd, qg_ref[...]).astype(BF16)
    k = _head_rms(seg(o + D_C, D_C), bd, kg_ref[...])
    v = seg(o + 2 * D_C, D_C)
    k_ref[...] = k.astype(BF16)
    v_ref[...] = v.astype(BF16)
    for new_ref, val in zip(out_refs[7:], (k, v)):
        seq = new_ref.shape[1]
        for s in range(new_ref.shape[0]):
            new_ref[s] = val[s * seq:(s + 1) * seq]
    o += 3 * D_C
    for j in range(3):
        gate_ref[:, j * D_MODEL:(j + 1) * D_MODEL] = jax.nn.sigmoid(seg(o + j * D_MODEL, D_MODEL)).astype(BF16)


def _proj(x, mods, layer, row_of_tile, g, w_in, bd, qg, kg, new_kv_seq=None, new_kv=()):
    n = x.shape[0]
    tok = lambda width: pl.BlockSpec((TM, width), lambda i: (i, 0))
    widths = [D_A, D_B, D_B, D_C, D_C, D_C, 3 * D_MODEL]
    out_shape = [jax.ShapeDtypeStruct((n, w), BF16) for w in widths]
    out_specs = [tok(w) for w in widths]
    if new_kv_seq is not None:
        seqs_per_tile = TM // new_kv_seq
        out_shape += [jax.ShapeDtypeStruct((n // new_kv_seq, DEPTH, new_kv_seq, D_C), F32)] * 2
        out_specs += [pl.BlockSpec((seqs_per_tile, None, new_kv_seq, D_C), lambda i: (i, layer, 0, 0))] * 2
    n_in = 7
    return pl.pallas_call(
        partial(_proj_kernel, len(new_kv)),
        grid=(n // TM,),
        in_specs=[tok(D_MODEL), _mods_spec(layer, row_of_tile),
                  _layer_resident((1, D_MODEL), layer), _layer_resident((D_MODEL, N_IN), layer),
                  _resident((D_C, D_C)), _layer_resident((1, D_C), layer), _layer_resident((1, D_C), layer)]
                 + [pl.BlockSpec(memory_space=pl.ANY)] * len(new_kv),
        out_specs=out_specs,
        out_shape=out_shape,
        input_output_aliases={n_in + j: len(widths) + j for j in range(len(new_kv))},
        compiler_params=_params(1),
        name="proj",
    )(x, mods, g, w_in, bd, qg, kg, *new_kv)


def _depthwise_taps(pad, rows, lanes, first, w_ref, n_taps):
    acc = None
    for r in range(SUBLANES):
        part = None
        for j in range(n_taps):
            if (first + j) % SUBLANES == r:
                src = pl.ds(rows.start + first + j - r, rows.stop - rows.start + SUBLANES)
                term = pad[src, lanes] * w_ref[j:j + 1, lanes]
                part = term if part is None else part + term
        if part is not None:
            part = part[r:r + rows.stop - rows.start]
            acc = part if acc is None else acc + part
    return acc


def _conv_lane_tile(t, lanes, apad, cpad, caw_ref, cab_ref, cbw_ref, ha_buf, hb_buf):
    block = t // CONV_BLOCKS
    anchors = []
    for start in range(0, t, block):
        rows = slice(start, start + block)
        ha = _depthwise_taps(apad, rows, lanes, HALO - CONV_A // 2, caw_ref, CONV_A) + cab_ref[:, lanes]
        hb = _depthwise_taps(cpad, rows, lanes, HALO - CONV_B // 2, cbw_ref, CONV_B)
        ha_buf[rows, lanes] = ha
        hb_buf[rows, lanes] = hb
        anchors.append(jnp.minimum(jnp.min(jnp.minimum(ha, hb), axis=0, keepdims=True), F32_MIN))
    return anchors


def _conv_outputs(ha, hb, bg, lng_ref, lnb_ref, wa_ref, wb_ref):
    mu = jnp.mean(ha, axis=-1, keepdims=True)
    hc = ha - mu
    ln = hc * lax.rsqrt(jnp.mean(hc * hc, axis=-1, keepdims=True) + EPS) * lng_ref[...] + lnb_ref[...]
    ya = _dot((ln * jax.nn.sigmoid(ln)).astype(BF16), wa_ref[...])
    yb = _dot((bg.astype(F32) * hb).astype(BF16), wb_ref[...])
    return ya, yb


def _merge(x_ref, gate_ref, g2, ya, yb, yc, wm_ref, o_ref):
    m = (gate_ref[:, 0:D_MODEL].astype(F32) * ya
         + gate_ref[:, D_MODEL:2 * D_MODEL].astype(F32) * yb
         + gate_ref[:, 2 * D_MODEL:3 * D_MODEL].astype(F32) * yc)
    o_ref[...] = x_ref[...] + g2 * _dot(m.astype(BF16), wm_ref[...])


def _head_lane_masks():
    lane = lax.broadcasted_iota(jnp.int32, (1, LANES), 1)
    return [(lane >= e * HEAD_DIM) & (lane < (e + 1) * HEAD_DIM) for e in range(HEADS_PER_LANE_TILE)]


def _mixer_scratch(t):
    return [pltpu.VMEM((t + 2 * HALO, D_A), F32), pltpu.VMEM((t + 2 * HALO, D_B), F32),
            pltpu.VMEM((t, D_A), F32), pltpu.VMEM((t, D_B), F32), pltpu.VMEM((t, D_C), BF16)]


def _mix_ctx_kernel(x_ref, m_ref, a_ref, bg_ref, ch_ref, q_ref, k_ref, v_ref, gate_ref,
                    caw_ref, cab_ref, lng_ref, lnb_ref, cbw_ref, wa_ref, wb_ref, wc_ref, wm_ref,
                    o_ref, apad, cpad, ha_buf, hb_buf, o_buf):
    t = x_ref.shape[0]
    zero_halo = jnp.zeros((HALO, D_A), F32)
    for pad, src in ((apad, a_ref), (cpad, ch_ref)):
        pad[0:HALO, :] = zero_halo
        pad[HALO:HALO + t, :] = src[...].astype(F32)
        pad[HALO + t:HALO + t + HALO, :] = zero_halo
    masks = _head_lane_masks()

    for mt in range(D_C // LANES):
        lanes = slice(mt * LANES, (mt + 1) * LANES)
        _conv_lane_tile(t, lanes, apad, cpad, caw_ref, cab_ref, cbw_ref, ha_buf, hb_buf)
        q2, k2, v2 = q_ref[:, lanes], k_ref[:, lanes], v_ref[:, lanes]
        o2 = None
        for e in range(HEADS_PER_LANE_TILE):
            s = _dot_t(jnp.where(masks[e], q2, jnp.zeros_like(q2)), k2)
            p = jnp.exp2(s - jnp.max(s, axis=-1, keepdims=True))
            inv = 1.0 / jnp.sum(p, axis=-1, keepdims=True)
            oe = _dot(p.astype(BF16), v2) * inv
            o2 = oe if o2 is None else jnp.where(masks[e], oe, o2)
        o_buf[:, lanes] = o2.astype(BF16)
    ya, yb = _conv_outputs(ha_buf[...], hb_buf[...], bg_ref[...], lng_ref, lnb_ref, wa_ref, wb_ref)
    yc = _dot(o_buf[...], wc_ref[...])
    _merge(x_ref, gate_ref, m_ref[5:6, :], ya, yb, yc, wm_ref, o_ref)


def _mix_lat_kernel(x_ref, m_ref, a_ref, ap_ref, an_ref, bg_ref, ch_ref, cp_ref, cn_ref,
                    q_ref, k_ref, v_ref, kc_ref, vc_ref, bias_ref, gate_ref,
                    caw_ref, cab_ref, lng_ref, lnb_ref, cbw_ref, wa_ref, wb_ref, wc_ref, wm_ref,
                    o_ref, apad, cpad, ha_buf, hb_buf, o_buf):
    t = x_ref.shape[0]
    i = pl.program_id(1)
    n_tiles = pl.num_programs(1)
    rows = n_tiles * ROWS_PER_TILE
    has_prev = (i > 0).astype(F32)
    has_next = (i < n_tiles - 1).astype(F32)
    for pad, prev, cur, nxt in ((apad, ap_ref, a_ref, an_ref), (cpad, cp_ref, ch_ref, cn_ref)):
        pad[0:HALO, :] = prev[...].astype(F32) * has_prev
        pad[HALO:HALO + t, :] = cur[...].astype(F32)
        pad[HALO + t:HALO + t + HALO, :] = nxt[...].astype(F32) * has_next
    masks = _head_lane_masks()
    n_win = WIN_ROWS * GRID_W
    win_start, bias_parity, bias_chunk = [], [], []
    for rr in range(ROWS_PER_TILE):
        r = i * ROWS_PER_TILE + rr
        rs = jnp.clip(r - WIN_ROWS // 2, 0, rows - WIN_ROWS)
        win_start.append(pl.multiple_of(rs * GRID_W, GRID_W))
        first_rel_row = WIN_ROWS - 1 - (r - rs)
        bias_parity.append(lax.rem(first_rel_row, ROWS_PER_CHUNK))
        bias_chunk.append(lax.div(first_rel_row, ROWS_PER_CHUNK))

    anchors_per_head = CONV_BLOCKS // HEADS_PER_LANE_TILE
    anchor_stride = ROWS_PER_TILE // anchors_per_head

    def local_bias(h, rr, anchors):
        chunks = [bias_ref[h, bias_parity[rr], bias_chunk[rr] + u] for u in range(n_win // LANES)]
        if rr % anchor_stride == 0:
            anchor = anchors[(h % HEADS_PER_LANE_TILE) * anchors_per_head + rr // anchor_stride]
            chunks[0] = jnp.maximum(chunks[0], anchor)
        return jnp.concatenate(chunks, axis=1)
    q_rows = [slice(rr * GRID_W, (rr + 1) * GRID_W) for rr in range(ROWS_PER_TILE)]
    for mt in range(D_C // LANES):
        lanes = slice(mt * LANES, (mt + 1) * LANES)
        anchors = _conv_lane_tile(t, lanes, apad, cpad, caw_ref, cab_ref, cbw_ref, ha_buf, hb_buf)
        q2 = q_ref[:, lanes]
        kc2 = kc_ref[:, lanes].astype(BF16)
        vc2 = vc_ref[:, lanes].astype(BF16)
        o2 = None
        for e in range(HEADS_PER_LANE_TILE):
            h = mt * HEADS_PER_LANE_TILE + e
            qm = jnp.where(masks[e], q2, jnp.zeros_like(q2))
            s_ctx = _dot_t(qm, kc2)
            s_loc = jnp.concatenate(
                [_dot_t(qm[q_rows[rr]], k_ref[pl.ds(win_start[rr], n_win), lanes]) + local_bias(h, rr, anchors)
                 for rr in range(ROWS_PER_TILE)], axis=0)
            mx = jnp.maximum(jnp.max(s_loc, axis=-1, keepdims=True), jnp.max(s_ctx, axis=-1, keepdims=True))
            p_l = jnp.exp2(s_loc - mx)
            p_c = jnp.exp2(s_ctx - mx)
            inv = 1.0 / (jnp.sum(p_l, axis=-1, keepdims=True) + jnp.sum(p_c, axis=-1, keepdims=True))
            p_l = p_l.astype(BF16)
            oe = jnp.concatenate(
                [_dot(p_l[q_rows[rr]], v_ref[pl.ds(win_start[rr], n_win), lanes])
                 for rr in range(ROWS_PER_TILE)], axis=0) + _dot(p_c.astype(BF16), vc2)
            oe = oe * inv
            o2 = oe if o2 is None else jnp.where(masks[e], oe, o2)
        o_buf[:, lanes] = o2.astype(BF16)
    ya, yb = _conv_outputs(ha_buf[...], hb_buf[...], bg_ref[...], lng_ref, lnb_ref, wa_ref, wb_ref)
    yc = _dot(o_buf[...], wc_ref[...])
    _merge(x_ref, gate_ref, m_ref[5:6, :], ya, yb, yc, wm_ref, o_ref)


def _mixer_weight_specs(layer):
    shapes = [(CONV_A, D_A), (1, D_A), (1, D_A), (1, D_A), (CONV_B, D_B),
              (D_A, D_MODEL), (D_B, D_MODEL), (D_C, D_MODEL), (D_MODEL, D_MODEL)]
    return [_layer_resident(shape, layer) for shape in shapes]


def _mix_ctx(x, mods, layer, pieces, weights, seq):
    n = x.shape[0]
    a, bg, ch, q, k, v, gates = pieces
    tok = lambda width: pl.BlockSpec((seq, width), lambda i: (i, 0))
    return pl.pallas_call(
        _mix_ctx_kernel,
        grid=(n // seq,),
        in_specs=[tok(D_MODEL), _mods_spec(layer, lambda i: 0),
                  tok(D_A), tok(D_B), tok(D_B), tok(D_C), tok(D_C), tok(D_C), tok(3 * D_MODEL)]
                 + _mixer_weight_specs(layer),
        out_specs=tok(D_MODEL),
        out_shape=jax.ShapeDtypeStruct((n, D_MODEL), F32),
        scratch_shapes=_mixer_scratch(seq),
        compiler_params=_params(1),
        name="mix_ctx",
    )(x, mods, a, bg, ch, q, k, v, gates, *weights)


def _mix_lat(x, mods, layer, pieces, cache_k, cache_v, bias, weights, batch, seq):
    a, bg, ch, q, k, v, gates = (p.reshape(batch, seq, p.shape[-1]) for p in pieces)
    t = ROWS_PER_TILE * GRID_W
    halo_per_tile = t // HALO
    n_halo = seq // HALO
    past = cache_k.shape[2]
    tok = lambda width: pl.BlockSpec((None, t, width), lambda b, i: (b, i, 0))
    prev = lambda width: pl.BlockSpec((None, HALO, width),
                                      lambda b, i: (b, jnp.maximum(i * halo_per_tile - 1, 0), 0))
    nxt = lambda width: pl.BlockSpec((None, HALO, width),
                                     lambda b, i: (b, jnp.minimum((i + 1) * halo_per_tile, n_halo - 1), 0))
    seq_resident = pl.BlockSpec((None, seq, D_C), lambda b, i: (b, 0, 0), pipeline_mode=pl.Buffered(1))
    cache = pl.BlockSpec((None, None, past, D_C), lambda b, i: (b, layer, 0, 0), pipeline_mode=pl.Buffered(1))
    out = pl.pallas_call(
        _mix_lat_kernel,
        grid=(batch, seq // t),
        in_specs=[tok(D_MODEL), _mods_spec(layer, lambda b, i: 1 + b),
                  tok(D_A), prev(D_A), nxt(D_A), tok(D_B), tok(D_B), prev(D_B), nxt(D_B),
                  tok(D_C), seq_resident, seq_resident, cache, cache,
                  _layer_resident((N_HEADS, ROWS_PER_CHUNK, N_BIAS_CHUNKS, GRID_W, LANES), layer),
                  tok(3 * D_MODEL)]
                 + _mixer_weight_specs(layer),
        out_specs=tok(D_MODEL),
        out_shape=jax.ShapeDtypeStruct((batch, seq, D_MODEL), F32),
        scratch_shapes=_mixer_scratch(t),
        compiler_params=_params(2),
        name="mix_lat",
    )(x.reshape(batch, seq, D_MODEL), mods, a, a, a, bg, ch, ch, ch, q, k, v,
      cache_k, cache_v, bias, gates, *weights)
    return out.reshape(batch * seq, D_MODEL)


def _local_bias_table(rpb):
    qc = np.arange(GRID_W)[:, None]
    kc = np.arange(GRID_W)[None, :]
    win_start = np.clip(qc - WIN_COLS // 2, 0, GRID_W - WIN_COLS)
    valid = (kc >= win_start) & (kc < win_start + WIN_COLS)
    assert np.all(np.abs(kc - qc)[valid] < WIN_COLS)
    pick = (valid[:, :, None] & (np.arange(2 * WIN_COLS - 1) == (kc - qc + WIN_COLS - 1)[:, :, None]))
    masked = np.where(valid, 0.0, NEG_INF).astype(np.float32)
    rel_rows = (np.arange(ROWS_PER_CHUNK)[:, None, None] + ROWS_PER_CHUNK * np.arange(N_BIAS_CHUNKS)[None, :, None]
                + np.arange(ROWS_PER_CHUNK)[None, None, :])
    chunk_rows = rpb[:, :, rel_rows, :] * LOG2E
    tab = jnp.einsum('lhpjdx,ckx->lhpjcdk', chunk_rows, jnp.asarray(pick, F32),
                     precision=lax.Precision.HIGHEST) + masked[:, None, :]
    return tab.reshape(rpb.shape[0], N_HEADS, ROWS_PER_CHUNK, N_BIAS_CHUNKS, GRID_W, LANES)


def kernel(x_prompt, x_sample, cache_k, cache_v, c, c_ctx, w_ada, b_ada, g_ff1, w_ff1_gate, w_ff1_up,
           w_ff1_down, g_mix, w_in, conv_a_w, conv_a_b, ln_a_g, ln_a_b, w_a_out, conv_b_w, w_b_out,
           q_norm_g, k_norm_g, rpb, w_c_out, w_merge, g_ff2, w_ff2_gate, w_ff2_up, w_ff2_down):
    batch, seq, _ = x_prompt.shape
    dec_batch, dec_seq, _ = x_sample.shape
    past = cache_k.shape[2]
    assert dec_batch + 1 <= MOD_ROWS and dec_seq % (ROWS_PER_TILE * GRID_W) == 0
    assert (batch * seq) % TM == 0 and dec_seq % TM == 0 and seq % HALO == 0 and TM % seq == 0

    cvecs = jnp.zeros((MOD_ROWS, D_MODEL), F32).at[0].set(c_ctx).at[1:1 + dec_batch].set(c)
    mods = _modulation(cvecs, w_ada, b_ada)
    ctx_row = lambda i: 0
    lat_row = lambda i: 1 + (i * TM) // dec_seq

    rows = lambda v: v.reshape(DEPTH, 1, -1)
    bf16 = lambda w: w.astype(BF16)
    head = np.arange(D_C) // HEAD_DIM
    bd = jnp.asarray((head[:, None] == head[None, :]) / HEAD_DIM, BF16)
    ff1 = (rows(g_ff1), bf16(w_ff1_gate), bf16(w_ff1_up), bf16(w_ff1_down))
    ff2 = (rows(g_ff2), bf16(w_ff2_gate), bf16(w_ff2_up), bf16(w_ff2_down))
    qg = rows(jnp.tile(q_norm_g, (1, N_HEADS)) * (HEAD_DIM ** -0.5 * LOG2E))
    kg = rows(jnp.tile(k_norm_g, (1, N_HEADS)))
    proj_w = (rows(g_mix), bf16(w_in), bd, qg, kg)
    mix_w = (conv_a_w, rows(conv_a_b), rows(ln_a_g), rows(ln_a_b), conv_b_w,
             bf16(w_a_out), bf16(w_b_out), bf16(w_c_out), bf16(w_merge))
    bias = _local_bias_table(rpb)
    ck = cache_k.reshape(dec_batch, DEPTH, past, D_C)
    cv = cache_v.reshape(dec_batch, DEPTH, past, D_C)

    h_ctx = x_prompt.reshape(batch * seq, D_MODEL)
    h_lat = x_sample.reshape(dec_batch * dec_seq, D_MODEL)
    new_kv = ()
    for l in range(DEPTH):
        h_ctx = _ffn(h_ctx, mods, l, 0, ctx_row, *ff1)
        *pieces, new_k, new_v = _proj(h_ctx, mods, l, ctx_row, *proj_w, new_kv_seq=seq, new_kv=new_kv)
        new_kv = (new_k, new_v)
        h_ctx = _mix_ctx(h_ctx, mods, l, pieces, mix_w, seq)
        h_ctx = _ffn(h_ctx, mods, l, 6, ctx_row, *ff2)

        h_lat = _ffn(h_lat, mods, l, 0, lat_row, *ff1)
        pieces = _proj(h_lat, mods, l, lat_row, *proj_w)
        h_lat = _mix_lat(h_lat, mods, l, pieces, ck, cv, bias, mix_w, dec_batch, dec_seq)
        h_lat = _ffn(h_lat, mods, l, 6, lat_row, *ff2)

    heads = lambda t: t.reshape(batch, DEPTH, seq, N_HEADS, HEAD_DIM)
    return (h_ctx.reshape(batch, seq, D_MODEL), h_lat.reshape(dec_batch, dec_seq, D_MODEL),
            heads(new_kv[0]), heads(new_kv[1]))
```

```python
from functools import partial

import numpy as np
import jax
import jax.numpy as jnp
from jax import lax
from jax.experimental import pallas as pl
from jax.experimental.pallas import tpu as pltpu

D_MODEL = 1024
DEPTH = 2
GRID_W = 64
D_A = 512
CONV_A = 31
D_B = 512
CONV_B = 3
N_HEADS = 8
HEAD_DIM = 64
D_C = N_HEADS * HEAD_DIM
WIN_ROWS = 8
WIN_COLS = 16
D_FF = 2816
N_MOD = 9
N_IN = 2 * D_A + 3 * D_B + 3 * D_C + 3 * D_MODEL
EPS = 1e-6
NEG_INF = -1e30
F32_MIN = float(np.finfo(np.float32).min)
LOG2E = 1.4426950408889634

LANES = 128
SUBLANES = 8
HEADS_PER_LANE_TILE = LANES // HEAD_DIM
N_BIAS_ROWS = 2 * WIN_ROWS - 1
ROWS_PER_CHUNK = LANES // GRID_W
N_BIAS_CHUNKS = (N_BIAS_ROWS - ROWS_PER_CHUNK) // ROWS_PER_CHUNK + 1
HALO = 16
CONV_BLOCKS = 2
MOD_ROWS = 16
VMEM_LIMIT = 60 * 1024 * 1024
TM = 512
ROWS_PER_TILE = 8

F32 = jnp.float32
BF16 = jnp.bfloat16


def _dot(a, b):
    return jnp.dot(a, b, preferred_element_type=F32)


def _dot_t(a, b):
    return lax.dot_general(a, b, (((1,), (1,)), ((), ())), preferred_element_type=F32)


def _resident(shape):
    zeros = (0,) * len(shape)
    return pl.BlockSpec(shape, lambda *_: zeros, pipeline_mode=pl.Buffered(1))


def _layer_resident(shape, layer):
    index = (layer,) + (0,) * len(shape)
    return pl.BlockSpec((None,) + tuple(shape), lambda *_: index, pipeline_mode=pl.Buffered(1))


def _mods_spec(layer, row_of_step):
    return pl.BlockSpec((None, None, N_MOD, D_MODEL), lambda *step: (layer, row_of_step(*step), 0, 0))


def _params(n_grid_dims):
    return pltpu.CompilerParams(dimension_semantics=("arbitrary",) * n_grid_dims,
                                vmem_limit_bytes=VMEM_LIMIT)


def _ada_ln(x, g, shift, scale):
    y = x * lax.rsqrt(jnp.mean(x * x, axis=-1, keepdims=True) + EPS)
    return (y * g) * (1.0 + scale) + shift


def _mods_kernel(c_ref, w_ref, b_ref, o_ref):
    c = c_ref[...]
    s = (c * jax.nn.sigmoid(c)).astype(BF16)
    o_ref[...] = _dot(s, w_ref[...].astype(BF16)) + b_ref[...]


def _modulation(cvecs, w_ada, b_ada):
    out = pl.pallas_call(
        _mods_kernel,
        grid=(DEPTH, N_MOD),
        in_specs=[pl.BlockSpec((MOD_ROWS, D_MODEL), lambda l, j: (0, 0)),
                  pl.BlockSpec((None, D_MODEL, D_MODEL), lambda l, j: (l, 0, j)),
                  pl.BlockSpec((None, 1, D_MODEL), lambda l, j: (l, 0, j))],
        out_specs=pl.BlockSpec((None, MOD_ROWS, D_MODEL), lambda l, j: (l, 0, j)),
        out_shape=jax.ShapeDtypeStruct((DEPTH, MOD_ROWS, N_MOD * D_MODEL), F32),
        compiler_params=_params(2),
        name="modulation",
    )(cvecs, w_ada, b_ada.reshape(DEPTH, 1, N_MOD * D_MODEL))
    return out.reshape(DEPTH, MOD_ROWS, N_MOD, D_MODEL)


def _ffn_kernel(mod0, x_ref, m_ref, g_ref, wg_ref, wu_ref, wd_ref, o_ref):
    x = x_ref[...]
    shift, scale, gate = (m_ref[mod0 + i:mod0 + i + 1, :] for i in range(3))
    u = _ada_ln(x, g_ref[...], shift, scale).astype(BF16)
    a = _dot(u, wg_ref[...])
    b = _dot(u, wu_ref[...])
    h = (a * jax.nn.sigmoid(a) * b).astype(BF16)
    o_ref[...] = x + (0.5 * gate) * _dot(h, wd_ref[...])


def _ffn(x, mods, layer, mod0, row_of_tile, g, wg, wu, wd):
    n = x.shape[0]
    return pl.pallas_call(
        partial(_ffn_kernel, mod0),
        grid=(n // TM,),
        in_specs=[pl.BlockSpec((TM, D_MODEL), lambda i: (i, 0)),
                  _mods_spec(layer, row_of_tile),
                  _layer_resident((1, D_MODEL), layer),
                  _layer_resident((D_MODEL, D_FF), layer), _layer_resident((D_MODEL, D_FF), layer),
                  _layer_resident((D_FF, D_MODEL), layer)],
        out_specs=pl.BlockSpec((TM, D_MODEL), lambda i: (i, 0)),
        out_shape=jax.ShapeDtypeStruct((n, D_MODEL), F32),
        compiler_params=_params(1),
        name="ffn",
    )(x, mods, g, wg, wu, wd)


def _head_rms(t, bd, g):
    ms = _dot((t * t).astype(BF16), bd)
    return t * lax.rsqrt(ms + EPS) * g


def _proj_kernel(n_alias, x_ref, m_ref, g_ref, w_ref, bd_ref, qg_ref, kg_ref, *refs):
    out_refs = refs[n_alias:]
    a_ref, bg_ref, ch_ref, q_ref, k_ref, v_ref, gate_ref = out_refs[:7]
    shift, scale = m_ref[3:4, :], m_ref[4:5, :]
    u = _ada_ln(x_ref[...], g_ref[...], shift, scale).astype(BF16)

    def seg(start, width):
        return _dot(u, w_ref[:, start:start + width])

    o = 0
    a_ref[...] = (seg(o, D_A) * jax.nn.sigmoid(seg(o + D_A, D_A))).astype(BF16)
    o += 2 * D_A
    bg_ref[...] = seg(o, D_B).astype(BF16)
    ch_ref[...] = (seg(o + D_B, D_B) * seg(o + 2 * D_B, D_B)).astype(BF16)
    o += 3 * D_B
    bd = bd_ref[...]
    q_ref[...] = _head_rms(seg(o, D_C), bd, qg_ref[...]).astype(BF16)
    k = _head_rms(seg(o + D_C, D_C), bd, kg_ref[...])
    v = seg(o + 2 * D_C, D_C)
    k_ref[...] = k.astype(BF16)
    v_ref[...] = v.astype(BF16)
    for new_ref, val in zip(out_refs[7:], (k, v)):
        seq = new_ref.shape[1]
        for s in range(new_ref.shape[0]):
            new_ref[s] = val[s * seq:(s + 1) * seq]
    o += 3 * D_C
    for j in range(3):
        gate_ref[:, j * D_MODEL:(j + 1) * D_MODEL] = jax.nn.sigmoid(seg(o + j * D_MODEL, D_MODEL)).astype(BF16)


def _proj(x, mods, layer, row_of_tile, g, w_in, bd, qg, kg, new_kv_seq=None, new_kv=()):
    n = x.shape[0]
    tok = lambda width: pl.BlockSpec((TM, width), lambda i: (i, 0))
    widths = [D_A, D_B, D_B, D_C, D_C, D_C, 3 * D_MODEL]
    out_shape = [jax.ShapeDtypeStruct((n, w), BF16) for w in widths]
    out_specs = [tok(w) for w in widths]
    if new_kv_seq is not None:
        seqs_per_tile = TM // new_kv_seq
        out_shape += [jax.ShapeDtypeStruct((n // new_kv_seq, DEPTH, new_kv_seq, D_C), F32)] * 2
        out_specs += [pl.BlockSpec((seqs_per_tile, None, new_kv_seq, D_C), lambda i: (i, layer, 0, 0))] * 2
    n_in = 7
    return pl.pallas_call(
        partial(_proj_kernel, len(new_kv)),
        grid=(n // TM,),
        in_specs=[tok(D_MODEL), _mods_spec(layer, row_of_tile),
                  _layer_resident((1, D_MODEL), layer), _layer_resident((D_MODEL, N_IN), layer),
                  _resident((D_C, D_C)), _layer_resident((1, D_C), layer), _layer_resident((1, D_C), layer)]
                 + [pl.BlockSpec(memory_space=pl.ANY)] * len(new_kv),
        out_specs=out_specs,
        out_shape=out_shape,
        input_output_aliases={n_in + j: len(widths) + j for j in range(len(new_kv))},
        compiler_params=_params(1),
        name="proj",
    )(x, mods, g, w_in, bd, qg, kg, *new_kv)


def _depthwise_taps(pad, rows, lanes, first, w_ref, n_taps):
    acc = None
    for r in range(SUBLANES):
        part = None
        for j in range(n_taps):
            if (first + j) % SUBLANES == r:
                src = pl.ds(rows.start + first + j - r, rows.stop - rows.start + SUBLANES)
                term = pad[src, lanes] * w_ref[j:j + 1, lanes]
                part = term if part is None else part + term
        if part is not None:
            part = part[r:r + rows.stop - rows.start]
            acc = part if acc is None else acc + part
    return acc


def _conv_lane_tile(t, lanes, apad, cpad, caw_ref, cab_ref, cbw_ref, ha_buf, hb_buf):
    block = t // CONV_BLOCKS
    anchors = []
    for start in range(0, t, block):
        rows = slice(start, start + block)
        ha = _depthwise_taps(apad, rows, lanes, HALO - CONV_A // 2, caw_ref, CONV_A) + cab_ref[:, lanes]
        hb = _depthwise_taps(cpad, rows, lanes, HALO - CONV_B // 2, cbw_ref, CONV_B)
        ha_buf[rows, lanes] = ha
        hb_buf[rows, lanes] = hb
        anchors.append(jnp.minimum(jnp.min(jnp.minimum(ha, hb), axis=0, keepdims=True), F32_MIN))
    return anchors


def _conv_outputs(ha, hb, bg, lng_ref, lnb_ref, wa_ref, wb_ref):
    mu = jnp.mean(ha, axis=-1, keepdims=True)
    hc = ha - mu
    ln = hc * lax.rsqrt(jnp.mean(hc * hc, axis=-1, keepdims=True) + EPS) * lng_ref[...] + lnb_ref[...]
    ya = _dot((ln * jax.nn.sigmoid(ln)).astype(BF16), wa_ref[...])
    yb = _dot((bg.astype(F32) * hb).astype(BF16), wb_ref[...])
    return ya, yb


def _merge(x_ref, gate_ref, g2, ya, yb, yc, wm_ref, o_ref):
    m = (gate_ref[:, 0:D_MODEL].astype(F32) * ya
         + gate_ref[:, D_MODEL:2 * D_MODEL].astype(F32) * yb
         + gate_ref[:, 2 * D_MODEL:3 * D_MODEL].astype(F32) * yc)
    o_ref[...] = x_ref[...] + g2 * _dot(m.astype(BF16), wm_ref[...])


def _head_lane_masks():
    lane = lax.broadcasted_iota(jnp.int32, (1, LANES), 1)
    return [(lane >= e * HEAD_DIM) & (lane < (e + 1) * HEAD_DIM) for e in range(HEADS_PER_LANE_TILE)]


def _mixer_scratch(t):
    return [pltpu.VMEM((t + 2 * HALO, D_A), F32), pltpu.VMEM((t + 2 * HALO, D_B), F32),
            pltpu.VMEM((t, D_A), F32), pltpu.VMEM((t, D_B), F32), pltpu.VMEM((t, D_C), BF16)]


def _mix_ctx_kernel(x_ref, m_ref, a_ref, bg_ref, ch_ref, q_ref, k_ref, v_ref, gate_ref,
                    caw_ref, cab_ref, lng_ref, lnb_ref, cbw_ref, wa_ref, wb_ref, wc_ref, wm_ref,
                    o_ref, apad, cpad, ha_buf, hb_buf, o_buf):
    t = x_ref.shape[0]
    zero_halo = jnp.zeros((HALO, D_A), F32)
    for pad, src in ((apad, a_ref), (cpad, ch_ref)):
        pad[0:HALO, :] = zero_halo
        pad[HALO:HALO + t, :] = src[...].astype(F32)
        pad[HALO + t:HALO + t + HALO, :] = zero_halo
    masks = _head_lane_masks()

    for mt in range(D_C // LANES):
        lanes = slice(mt * LANES, (mt + 1) * LANES)
        _conv_lane_tile(t, lanes, apad, cpad, caw_ref, cab_ref, cbw_ref, ha_buf, hb_buf)
        q2, k2, v2 = q_ref[:, lanes], k_ref[:, lanes], v_ref[:, lanes]
        o2 = None
        for e in range(HEADS_PER_LANE_TILE):
            s = _dot_t(jnp.where(masks[e], q2, jnp.zeros_like(q2)), k2)
            p = jnp.exp2(s - jnp.max(s, axis=-1, keepdims=True))
            inv = 1.0 / jnp.sum(p, axis=-1, keepdims=True)
            oe = _dot(p.astype(BF16), v2) * inv
            o2 = oe if o2 is None else jnp.where(masks[e], oe, o2)
        o_buf[:, lanes] = o2.astype(BF16)
    ya, yb = _conv_outputs(ha_buf[...], hb_buf[...], bg_ref[...], lng_ref, lnb_ref, wa_ref, wb_ref)
    yc = _dot(o_buf[...], wc_ref[...])
    _merge(x_ref, gate_ref, m_ref[5:6, :], ya, yb, yc, wm_ref, o_ref)


def _mix_lat_kernel(x_ref, m_ref, a_ref, ap_ref, an_ref, bg_ref, ch_ref, cp_ref, cn_ref,
                    q_ref, k_ref, v_ref, kc_ref, vc_ref, bias_ref, gate_ref,
                    caw_ref, cab_ref, lng_ref, lnb_ref, cbw_ref, wa_ref, wb_ref, wc_ref, wm_ref,
                    o_ref, apad, cpad, ha_buf, hb_buf, o_buf):
    t = x_ref.shape[0]
    i = pl.program_id(1)
    n_tiles = pl.num_programs(1)
    rows = n_tiles * ROWS_PER_TILE
    has_prev = (i > 0).astype(F32)
    has_next = (i < n_tiles - 1).astype(F32)
    for pad, prev, cur, nxt in ((apad, ap_ref, a_ref, an_ref), (cpad, cp_ref, ch_ref, cn_ref)):
        pad[0:HALO, :] = prev[...].astype(F32) * has_prev
        pad[HALO:HALO + t, :] = cur[...].astype(F32)
        pad[HALO + t:HALO + t + HALO, :] = nxt[...].astype(F32) * has_next
    masks = _head_lane_masks()
    n_win = WIN_ROWS * GRID_W
    win_start, bias_parity, bias_chunk = [], [], []
    for rr in range(ROWS_PER_TILE):
        r = i * ROWS_PER_TILE + rr
        rs = jnp.clip(r - WIN_ROWS // 2, 0, rows - WIN_ROWS)
        win_start.append(pl.multiple_of(rs * GRID_W, GRID_W))
        first_rel_row = WIN_ROWS - 1 - (r - rs)
        bias_parity.append(lax.rem(first_rel_row, ROWS_PER_CHUNK))
        bias_chunk.append(lax.div(first_rel_row, ROWS_PER_CHUNK))

    anchors_per_head = CONV_BLOCKS // HEADS_PER_LANE_TILE
    anchor_stride = ROWS_PER_TILE // anchors_per_head

    def local_bias(h, rr, anchors):
        chunks = [bias_ref[h, bias_parity[rr], bias_chunk[rr] + u] for u in range(n_win // LANES)]
        if rr % anchor_stride == 0:
            anchor = anchors[(h % HEADS_PER_LANE_TILE) * anchors_per_head + rr // anchor_stride]
            chunks[0] = jnp.maximum(chunks[0], anchor)
        return jnp.concatenate(chunks, axis=1)
    q_rows = [slice(rr * GRID_W, (rr + 1) * GRID_W) for rr in range(ROWS_PER_TILE)]
    for mt in range(D_C // LANES):
        lanes = slice(mt * LANES, (mt + 1) * LANES)
        anchors = _conv_lane_tile(t, lanes, apad, cpad, caw_ref, cab_ref, cbw_ref, ha_buf, hb_buf)
        q2 = q_ref[:, lanes]
        kc2 = kc_ref[:, lanes].astype(BF16)
        vc2 = vc_ref[:, lanes].astype(BF16)
        o2 = None
        for e in range(HEADS_PER_LANE_TILE):
            h = mt * HEADS_PER_LANE_TILE + e
            qm = jnp.where(masks[e], q2, jnp.zeros_like(q2))
            s_ctx = _dot_t(qm, kc2)
            s_loc = jnp.concatenate(
                [_dot_t(qm[q_rows[rr]], k_ref[pl.ds(win_start[rr], n_win), lanes]) + local_bias(h, rr, anchors)
                 for rr in range(ROWS_PER_TILE)], axis=0)
            mx = jnp.maximum(jnp.max(s_loc, axis=-1, keepdims=True), jnp.max(s_ctx, axis=-1, keepdims=True))
            p_l = jnp.exp2(s_loc - mx)
            p_c = jnp.exp2(s_ctx - mx)
            inv = 1.0 / (jnp.sum(p_l, axis=-1, keepdims=True) + jnp.sum(p_c, axis=-1, keepdims=True))
            p_l = p_l.astype(BF16)
            oe = jnp.concatenate(
                [_dot(p_l[q_rows[rr]], v_ref[pl.ds(win_start[rr], n_win), lanes])
                 for rr in range(ROWS_PER_TILE)], axis=0) + _dot(p_c.astype(BF16), vc2)
            oe = oe * inv
            o2 = oe if o2 is None else jnp.where(masks[e], oe, o2)
        o_buf[:, lanes] = o2.astype(BF16)
    ya, yb = _conv_outputs(ha_buf[...], hb_buf[...], bg_ref[...], lng_ref, lnb_ref, wa_ref, wb_ref)
    yc = _dot(o_buf[...], wc_ref[...])
    _merge(x_ref, gate_ref, m_ref[5:6, :], ya, yb, yc, wm_ref, o_ref)


def _mixer_weight_specs(layer):
    shapes = [(CONV_A, D_A), (1, D_A), (1, D_A), (1, D_A), (CONV_B, D_B),
              (D_A, D_MODEL), (D_B, D_MODEL), (D_C, D_MODEL), (D_MODEL, D_MODEL)]
    return [_layer_resident(shape, layer) for shape in shapes]


def _mix_ctx(x, mods, layer, pieces, weights, seq):
    n = x.shape[0]
    a, bg, ch, q, k, v, gates = pieces
    tok = lambda width: pl.BlockSpec((seq, width), lambda i: (i, 0))
    return pl.pallas_call(
        _mix_ctx_kernel,
        grid=(n // seq,),
        in_specs=[tok(D_MODEL), _mods_spec(layer, lambda i: 0),
                  tok(D_A), tok(D_B), tok(D_B), tok(D_C), tok(D_C), tok(D_C), tok(3 * D_MODEL)]
                 + _mixer_weight_specs(layer),
        out_specs=tok(D_MODEL),
        out_shape=jax.ShapeDtypeStruct((n, D_MODEL), F32),
        scratch_shapes=_mixer_scratch(seq),
        compiler_params=_params(1),
        name="mix_ctx",
    )(x, mods, a, bg, ch, q, k, v, gates, *weights)


def _mix_lat(x, mods, layer, pieces, cache_k, cache_v, bias, weights, batch, seq):
    a, bg, ch, q, k, v, gates = (p.reshape(batch, seq, p.shape[-1]) for p in pieces)
    t = ROWS_PER_TILE * GRID_W
    halo_per_tile = t // HALO
    n_halo = seq // HALO
    past = cache_k.shape[2]
    tok = lambda width: pl.BlockSpec((None, t, width), lambda b, i: (b, i, 0))
    prev = lambda width: pl.BlockSpec((None, HALO, width),
                                      lambda b, i: (b, jnp.maximum(i * halo_per_tile - 1, 0), 0))
    nxt = lambda width: pl.BlockSpec((None, HALO, width),
                                     lambda b, i: (b, jnp.minimum((i + 1) * halo_per_tile, n_halo - 1), 0))
    seq_resident = pl.BlockSpec((None, seq, D_C), lambda b, i: (b, 0, 0), pipeline_mode=pl.Buffered(1))
    cache = pl.BlockSpec((None, None, past, D_C), lambda b, i: (b, layer, 0, 0), pipeline_mode=pl.Buffered(1))
    out = pl.pallas_call(
        _mix_lat_kernel,
        grid=(batch, seq // t),
        in_specs=[tok(D_MODEL), _mods_spec(layer, lambda b, i: 1 + b),
                  tok(D_A), prev(D_A), nxt(D_A), tok(D_B), tok(D_B), prev(D_B), nxt(D_B),
                  tok(D_C), seq_resident, seq_resident, cache, cache,
                  _layer_resident((N_HEADS, ROWS_PER_CHUNK, N_BIAS_CHUNKS, GRID_W, LANES), layer),
                  tok(3 * D_MODEL)]
                 + _mixer_weight_specs(layer),
        out_specs=tok(D_MODEL),
        out_shape=jax.ShapeDtypeStruct((batch, seq, D_MODEL), F32),
        scratch_shapes=_mixer_scratch(t),
        compiler_params=_params(2),
        name="mix_lat",
    )(x.reshape(batch, seq, D_MODEL), mods, a, a, a, bg, ch, ch, ch, q, k, v,
      cache_k, cache_v, bias, gates, *weights)
    return out.reshape(batch * seq, D_MODEL)


def _local_bias_table(rpb):
    qc = np.arange(GRID_W)[:, None]
    kc = np.arange(GRID_W)[None, :]
    win_start = np.clip(qc - WIN_COLS // 2, 0, GRID_W - WIN_COLS)
    valid = (kc >= win_start) & (kc < win_start + WIN_COLS)
    assert np.all(np.abs(kc - qc)[valid] < WIN_COLS)
    pick = (valid[:, :, None] & (np.arange(2 * WIN_COLS - 1) == (kc - qc + WIN_COLS - 1)[:, :, None]))
    masked = np.where(valid, 0.0, NEG_INF).astype(np.float32)
    rel_rows = (np.arange(ROWS_PER_CHUNK)[:, None, None] + ROWS_PER_CHUNK * np.arange(N_BIAS_CHUNKS)[None, :, None]
                + np.arange(ROWS_PER_CHUNK)[None, None, :])
    chunk_rows = rpb[:, :, rel_rows, :] * LOG2E
    tab = jnp.einsum('lhpjdx,ckx->lhpjcdk', chunk_rows, jnp.asarray(pick, F32),
                     precision=lax.Precision.HIGHEST) + masked[:, None, :]
    return tab.reshape(rpb.shape[0], N_HEADS, ROWS_PER_CHUNK, N_BIAS_CHUNKS, GRID_W, LANES)


def kernel(x_prompt, x_sample, cache_k, cache_v, c, c_ctx, w_ada, b_ada, g_ff1, w_ff1_gate, w_ff1_up,
           w_ff1_down, g_mix, w_in, conv_a_w, conv_a_b, ln_a_g, ln_a_b, w_a_out, conv_b_w, w_b_out,
           q_norm_g, k_norm_g, rpb, w_c_out, w_merge, g_ff2, w_ff2_gate, w_ff2_up, w_ff2_down):
    batch, seq, _ = x_prompt.shape
    dec_batch, dec_seq, _ = x_sample.shape
    past = cache_k.shape[2]
    assert dec_batch + 1 <= MOD_ROWS and dec_seq % (ROWS_PER_TILE * GRID_W) == 0
    assert (batch * seq) % TM == 0 and dec_seq % TM == 0 and seq % HALO == 0 and TM % seq == 0

    cvecs = jnp.zeros((MOD_ROWS, D_MODEL), F32).at[0].set(c_ctx).at[1:1 + dec_batch].set(c)
    mods = _modulation(cvecs, w_ada, b_ada)
    ctx_row = lambda i: 0
    lat_row = lambda i: 1 + (i * TM) // dec_seq

    rows = lambda v: v.reshape(DEPTH, 1, -1)
    bf16 = lambda w: w.astype(BF16)
    head = np.arange(D_C) // HEAD_DIM
    bd = jnp.asarray((head[:, None] == head[None, :]) / HEAD_DIM, BF16)
    ff1 = (rows(g_ff1), bf16(w_ff1_gate), bf16(w_ff1_up), bf16(w_ff1_down))
    ff2 = (rows(g_ff2), bf16(w_ff2_gate), bf16(w_ff2_up), bf16(w_ff2_down))
    qg = rows(jnp.tile(q_norm_g, (1, N_HEADS)) * (HEAD_DIM ** -0.5 * LOG2E))
    kg = rows(jnp.tile(k_norm_g, (1, N_HEADS)))
    proj_w = (rows(g_mix), bf16(w_in), bd, qg, kg)
    mix_w = (conv_a_w, rows(conv_a_b), rows(ln_a_g), rows(ln_a_b), conv_b_w,
             bf16(w_a_out), bf16(w_b_out), bf16(w_c_out), bf16(w_merge))
    bias = _local_bias_table(rpb)
    ck = cache_k.reshape(dec_batch, DEPTH, past, D_C)
    cv = cache_v.reshape(dec_batch, DEPTH, past, D_C)

    h_ctx = x_prompt.reshape(batch * seq, D_MODEL)
    h_lat = x_sample.reshape(dec_batch * dec_seq, D_MODEL)
    new_kv = ()
    for l in range(DEPTH):
        h_ctx = _ffn(h_ctx, mods, l, 0, ctx_row, *ff1)
        *pieces, new_k, new_v = _proj(h_ctx, mods, l, ctx_row, *proj_w, new_kv_seq=seq, new_kv=new_kv)
        new_kv = (new_k, new_v)
        h_ctx = _mix_ctx(h_ctx, mods, l, pieces, mix_w, seq)
        h_ctx = _ffn(h_ctx, mods, l, 6, ctx_row, *ff2)

        h_lat = _ffn(h_lat, mods, l, 0, lat_row, *ff1)
        pieces = _proj(h_lat, mods, l, lat_row, *proj_w)
        h_lat = _mix_lat(h_lat, mods, l, pieces, ck, cv, bias, mix_w, dec_batch, dec_seq)
        h_lat = _ffn(h_lat, mods, l, 6, lat_row, *ff2)

    heads = lambda t: t.reshape(batch, DEPTH, seq, N_HEADS, HEAD_DIM)
    return (h_ctx.reshape(batch, seq, D_MODEL), h_lat.reshape(dec_batch, dec_seq, D_MODEL),
            heads(new_kv[0]), heads(new_kv[1]))
```

```python
from functools import partial

import numpy as np
import jax
import jax.numpy as jnp
from jax import lax
from jax.experimental import pallas as pl
from jax.experimental.pallas import tpu as pltpu

D_MODEL = 1024
DEPTH = 2
GRID_W = 64
D_A = 512
CONV_A = 31
D_B = 512
CONV_B = 3
N_HEADS = 8
HEAD_DIM = 64
D_C = N_HEADS * HEAD_DIM
WIN_ROWS = 8
WIN_COLS = 16
D_FF = 2816
N_MOD = 9
N_IN = 2 * D_A + 3 * D_B + 3 * D_C + 3 * D_MODEL
EPS = 1e-6
NEG_INF = -1e30
F32_MIN = float(np.finfo(np.float32).min)
LOG2E = 1.4426950408889634

LANES = 128
SUBLANES = 8
HEADS_PER_LANE_TILE = LANES // HEAD_DIM
N_BIAS_ROWS = 2 * WIN_ROWS - 1
ROWS_PER_CHUNK = LANES // GRID_W
N_BIAS_CHUNKS = (N_BIAS_ROWS - ROWS_PER_CHUNK) // ROWS_PER_CHUNK + 1
HALO = 16
CONV_BLOCKS = 8
MOD_ROWS = 16
VEC_CONV_A_B, VEC_LN_G, VEC_LN_B, VEC_CONV_B_W = 0, 1, 2, 3
assert VEC_CONV_B_W + CONV_B <= SUBLANES and D_A == D_B
VMEM_LIMIT = 60 * 1024 * 1024
TM = 512
ROWS_PER_TILE = 8

F32 = jnp.float32
BF16 = jnp.bfloat16


def _dot(a, b):
    return jnp.dot(a, b, preferred_element_type=F32)


def _dot_t(a, b):
    return lax.dot_general(a, b, (((1,), (1,)), ((), ())), preferred_element_type=F32)


def _resident(shape):
    zeros = (0,) * len(shape)
    return pl.BlockSpec(shape, lambda *_: zeros, pipeline_mode=pl.Buffered(1))


def _layer_resident(shape, layer):
    index = (layer,) + (0,) * len(shape)
    return pl.BlockSpec((None,) + tuple(shape), lambda *_: index, pipeline_mode=pl.Buffered(1))


def _mods_spec(layer, row_of_step):
    return pl.BlockSpec((None, None, N_MOD, D_MODEL), lambda *step: (layer, row_of_step(*step), 0, 0))


def _params(n_grid_dims):
    return pltpu.CompilerParams(dimension_semantics=("arbitrary",) * n_grid_dims,
                                vmem_limit_bytes=VMEM_LIMIT)


def _ada_ln(x, g, shift, scale):
    y = x * lax.rsqrt(jnp.mean(x * x, axis=-1, keepdims=True) + EPS)
    return (y * g) * (1.0 + scale) + shift


def _mods_kernel(c_ref, w_ref, b_ref, o_ref):
    c = c_ref[...]
    s = (c * jax.nn.sigmoid(c)).astype(BF16)
    o_ref[...] = _dot(s, w_ref[...].astype(BF16)) + b_ref[...]


def _modulation(cvecs, w_ada, b_ada):
    out = pl.pallas_call(
        _mods_kernel,
        grid=(DEPTH, N_MOD),
        in_specs=[pl.BlockSpec((MOD_ROWS, D_MODEL), lambda l, j: (0, 0)),
                  pl.BlockSpec((None, D_MODEL, D_MODEL), lambda l, j: (l, 0, j)),
                  pl.BlockSpec((None, 1, D_MODEL), lambda l, j: (l, 0, j))],
        out_specs=pl.BlockSpec((None, MOD_ROWS, D_MODEL), lambda l, j: (l, 0, j)),
        out_shape=jax.ShapeDtypeStruct((DEPTH, MOD_ROWS, N_MOD * D_MODEL), F32),
        compiler_params=_params(2),
        name="modulation",
    )(cvecs, w_ada, b_ada.reshape(DEPTH, 1, N_MOD * D_MODEL))
    return out.reshape(DEPTH, MOD_ROWS, N_MOD, D_MODEL)


def _ffn_kernel(mod0, x_ref, m_ref, g_ref, wg_ref, wu_ref, wd_ref, o_ref):
    x = x_ref[...]
    shift, scale, gate = (m_ref[mod0 + i:mod0 + i + 1, :] for i in range(3))
    u = _ada_ln(x, g_ref[0:1, :], shift, scale).astype(BF16)
    a = _dot(u, wg_ref[...])
    b = _dot(u, wu_ref[...])
    h = (a * jax.nn.sigmoid(a) * b).astype(BF16)
    o_ref[...] = x + (0.5 * gate) * _dot(h, wd_ref[...])


def _ffn(x, mods, layer, mod0, row_of_tile, g, wg, wu, wd):
    n = x.shape[0]
    return pl.pallas_call(
        partial(_ffn_kernel, mod0),
        grid=(n // TM,),
        in_specs=[pl.BlockSpec((TM, D_MODEL), lambda i: (i, 0)),
                  _mods_spec(layer, row_of_tile),
                  _layer_resident((SUBLANES, D_MODEL), layer),
                  _layer_resident((D_MODEL, D_FF), layer), _layer_resident((D_MODEL, D_FF), layer),
                  _layer_resident((D_FF, D_MODEL), layer)],
        out_specs=pl.BlockSpec((TM, D_MODEL), lambda i: (i, 0)),
        out_shape=jax.ShapeDtypeStruct((n, D_MODEL), F32),
        compiler_params=_params(1),
        name="ffn",
    )(x, mods, g, wg, wu, wd)


def _head_rms(t, bd, g):
    ms = _dot((t * t).astype(BF16), bd)
    return t * lax.rsqrt(ms + EPS) * g


def _proj_kernel(layer, n_alias, x_ref, m_ref, g_ref, w_ref, bd_ref, qkg_ref, *refs):
    out_refs = refs[n_alias:]
    a_ref, bg_ref, ch_ref, q_ref, k_ref, v_ref, gate_ref = out_refs[:7]
    shift, scale = m_ref[3:4, :], m_ref[4:5, :]
    u = _ada_ln(x_ref[...], g_ref[0:1, :], shift, scale).astype(BF16)

    def seg(start, width):
        return _dot(u, w_ref[:, start:start + width])

    o = 0
    a_ref[...] = (seg(o, D_A) * jax.nn.sigmoid(seg(o + D_A, D_A))).astype(BF16)
    o += 2 * D_A
    bg_ref[...] = seg(o, D_B).astype(BF16)
    ch_ref[...] = (seg(o + D_B, D_B) * seg(o + 2 * D_B, D_B)).astype(BF16)
    o += 3 * D_B
    bd = bd_ref[...]
    q_ref[...] = _head_rms(seg(o, D_C), bd, qkg_ref[0:1, :]).astype(BF16)
    k = _head_rms(seg(o + D_C, D_C), bd, qkg_ref[1:2, :])
    v = seg(o + 2 * D_C, D_C)
    k_ref[...] = k.astype(BF16)
    v_ref[...] = v.astype(BF16)
    for new_ref, val in zip(out_refs[7:], (k, v)):
        seq = new_ref.shape[-2]
        for s in range(new_ref.shape[0]):
            rows = val[s * seq:(s + 1) * seq]
            if n_alias:
                new_ref[s] = rows
            else:
                for l in range(new_ref.shape[1]):
                    new_ref[s, l] = rows if l == layer else jnp.zeros_like(rows)
    o += 3 * D_C
    for j in range(3):
        gate_ref[:, j * D_MODEL:(j + 1) * D_MODEL] = jax.nn.sigmoid(seg(o + j * D_MODEL, D_MODEL)).astype(BF16)


def _proj(x, mods, layer, row_of_tile, g, w_in, bd, qkg, new_kv_seq=None, new_kv=()):
    n = x.shape[0]
    tok = lambda width: pl.BlockSpec((TM, width), lambda i: (i, 0))
    widths = [D_A, D_B, D_B, D_C, D_C, D_C, 3 * D_MODEL]
    out_shape = [jax.ShapeDtypeStruct((n, w), BF16) for w in widths]
    out_specs = [tok(w) for w in widths]
    if new_kv_seq is not None:
        seqs_per_tile = TM // new_kv_seq
        out_shape += [jax.ShapeDtypeStruct((n // new_kv_seq, DEPTH, new_kv_seq, D_C), F32)] * 2
        if new_kv:
            kv_spec = pl.BlockSpec((seqs_per_tile, None, new_kv_seq, D_C), lambda i: (i, layer, 0, 0))
        else:
            kv_spec = pl.BlockSpec((seqs_per_tile, DEPTH, new_kv_seq, D_C), lambda i: (i, 0, 0, 0))
        out_specs += [kv_spec] * 2
    n_in = 6
    return pl.pallas_call(
        partial(_proj_kernel, layer, len(new_kv)),
        grid=(n // TM,),
        in_specs=[tok(D_MODEL), _mods_spec(layer, row_of_tile),
                  _layer_resident((SUBLANES, D_MODEL), layer), _layer_resident((D_MODEL, N_IN), layer),
                  _resident((D_C, D_C)), _layer_resident((SUBLANES, D_C), layer)]
                 + [pl.BlockSpec(memory_space=pl.ANY)] * len(new_kv),
        out_specs=out_specs,
        out_shape=out_shape,
        input_output_aliases={n_in + j: len(widths) + j for j in range(len(new_kv))},
        compiler_params=_params(1),
        name="proj",
    )(x, mods, g, w_in, bd, qkg, *new_kv)


def _depthwise_taps(pad, rows, lanes, first, w_ref, n_taps, w_row0=0):
    acc = None
    for r in range(SUBLANES):
        part = None
        for j in range(n_taps):
            if (first + j) % SUBLANES == r:
                src = pl.ds(rows.start + first + j - r, rows.stop - rows.start + SUBLANES)
                term = pad[src, lanes] * w_ref[w_row0 + j:w_row0 + j + 1, lanes]
                part = term if part is None else part + term
        if part is not None:
            part = part[r:r + rows.stop - rows.start]
            acc = part if acc is None else acc + part
    return acc


def _conv_lane_tile(t, lanes, apad, cpad, caw_ref, vec_ref, ha_buf, hb_buf):
    block = t // CONV_BLOCKS
    anchors = []
    for start in range(0, t, block):
        rows = slice(start, start + block)
        ha = (_depthwise_taps(apad, rows, lanes, HALO - CONV_A // 2, caw_ref, CONV_A)
              + vec_ref[VEC_CONV_A_B:VEC_CONV_A_B + 1, lanes])
        hb = _depthwise_taps(cpad, rows, lanes, HALO - CONV_B // 2, vec_ref, CONV_B, VEC_CONV_B_W)
        ha_buf[rows, lanes] = ha
        hb_buf[rows, lanes] = hb
        anchors.append(jnp.minimum(jnp.min(jnp.minimum(ha, hb), axis=0, keepdims=True), F32_MIN))
    return anchors


def _conv_outputs(ha, hb, bg, vec_ref, wa_ref, wb_ref):
    mu = jnp.mean(ha, axis=-1, keepdims=True)
    hc = ha - mu
    ln = (hc * lax.rsqrt(jnp.mean(hc * hc, axis=-1, keepdims=True) + EPS) * vec_ref[VEC_LN_G:VEC_LN_G + 1, :]
          + vec_ref[VEC_LN_B:VEC_LN_B + 1, :])
    ya = _dot((ln * jax.nn.sigmoid(ln)).astype(BF16), wa_ref[...])
    yb = _dot((bg.astype(F32) * hb).astype(BF16), wb_ref[...])
    return ya, yb


def _merge(x_ref, gate_ref, g2, ya, yb, yc, wm_ref, o_ref):
    m = (gate_ref[:, 0:D_MODEL].astype(F32) * ya
         + gate_ref[:, D_MODEL:2 * D_MODEL].astype(F32) * yb
         + gate_ref[:, 2 * D_MODEL:3 * D_MODEL].astype(F32) * yc)
    o_ref[...] = x_ref[...] + g2 * _dot(m.astype(BF16), wm_ref[...])


def _head_lane_masks():
    lane = lax.broadcasted_iota(jnp.int32, (1, LANES), 1)
    return [(lane >= e * HEAD_DIM) & (lane < (e + 1) * HEAD_DIM) for e in range(HEADS_PER_LANE_TILE)]


def _mixer_scratch(t):
    return [pltpu.VMEM((t + 2 * HALO, D_A), F32), pltpu.VMEM((t + 2 * HALO, D_B), F32),
            pltpu.VMEM((t, D_A), F32), pltpu.VMEM((t, D_B), F32), pltpu.VMEM((t, D_C), BF16)]


def _mix_ctx_kernel(x_ref, m_ref, a_ref, bg_ref, ch_ref, q_ref, k_ref, v_ref, gate_ref,
                    caw_ref, vec_ref, wa_ref, wb_ref, wc_ref, wm_ref,
                    o_ref, apad, cpad, ha_buf, hb_buf, o_buf):
    t = x_ref.shape[0]
    zero_halo = jnp.zeros((HALO, D_A), F32)
    for pad, src in ((apad, a_ref), (cpad, ch_ref)):
        pad[0:HALO, :] = zero_halo
        pad[HALO:HALO + t, :] = src[...].astype(F32)
        pad[HALO + t:HALO + t + HALO, :] = zero_halo
    masks = _head_lane_masks()

    for mt in range(D_C // LANES):
        lanes = slice(mt * LANES, (mt + 1) * LANES)
        _conv_lane_tile(t, lanes, apad, cpad, caw_ref, vec_ref, ha_buf, hb_buf)
        q2, k2, v2 = q_ref[:, lanes], k_ref[:, lanes], v_ref[:, lanes]
        o2 = None
        for e in range(HEADS_PER_LANE_TILE):
            s = _dot_t(jnp.where(masks[e], q2, jnp.zeros_like(q2)), k2)
            p = jnp.exp2(s - jnp.max(s, axis=-1, keepdims=True))
            inv = 1.0 / jnp.sum(p, axis=-1, keepdims=True)
            oe = _dot(p.astype(BF16), v2) * inv
            o2 = oe if o2 is None else jnp.where(masks[e], oe, o2)
        o_buf[:, lanes] = o2.astype(BF16)
    ya, yb = _conv_outputs(ha_buf[...], hb_buf[...], bg_ref[...], vec_ref, wa_ref, wb_ref)
    yc = _dot(o_buf[...], wc_ref[...])
    _merge(x_ref, gate_ref, m_ref[5:6, :], ya, yb, yc, wm_ref, o_ref)


def _mix_lat_kernel(x_ref, m_ref, a_ref, ap_ref, an_ref, bg_ref, ch_ref, cp_ref, cn_ref,
                    q_ref, k_ref, v_ref, kc_ref, vc_ref, bias_ref, gate_ref,
                    caw_ref, vec_ref, wa_ref, wb_ref, wc_ref, wm_ref,
                    o_ref, apad, cpad, ha_buf, hb_buf, o_buf):
    t = x_ref.shape[0]
    i = pl.program_id(1)
    n_tiles = pl.num_programs(1)
    rows = n_tiles * ROWS_PER_TILE
    has_prev = (i > 0).astype(F32)
    has_next = (i < n_tiles - 1).astype(F32)
    for pad, prev, cur, nxt in ((apad, ap_ref, a_ref, an_ref), (cpad, cp_ref, ch_ref, cn_ref)):
        pad[0:HALO, :] = prev[...].astype(F32) * has_prev
        pad[HALO:HALO + t, :] = cur[...].astype(F32)
        pad[HALO + t:HALO + t + HALO, :] = nxt[...].astype(F32) * has_next
    masks = _head_lane_masks()
    n_win = WIN_ROWS * GRID_W
    win_start, bias_parity, bias_chunk = [], [], []
    for rr in range(ROWS_PER_TILE):
        r = i * ROWS_PER_TILE + rr
        rs = jnp.clip(r - WIN_ROWS // 2, 0, rows - WIN_ROWS)
        win_start.append(pl.multiple_of(rs * GRID_W, GRID_W))
        first_rel_row = WIN_ROWS - 1 - (r - rs)
        bias_parity.append(lax.rem(first_rel_row, ROWS_PER_CHUNK))
        bias_chunk.append(lax.div(first_rel_row, ROWS_PER_CHUNK))

    anchors_per_head = CONV_BLOCKS // HEADS_PER_LANE_TILE
    anchor_stride = ROWS_PER_TILE // anchors_per_head

    def local_bias(h, rr, anchors):
        chunks = [bias_ref[h, bias_parity[rr], bias_chunk[rr] + u] for u in range(n_win // LANES)]
        if rr % anchor_stride == 0:
            anchor = anchors[(h % HEADS_PER_LANE_TILE) * anchors_per_head + rr // anchor_stride]
            chunks[0] = jnp.maximum(chunks[0], anchor)
        return jnp.concatenate(chunks, axis=1)
    q_rows = [slice(rr * GRID_W, (rr + 1) * GRID_W) for rr in range(ROWS_PER_TILE)]
    for mt in range(D_C // LANES):
        lanes = slice(mt * LANES, (mt + 1) * LANES)
        anchors = _conv_lane_tile(t, lanes, apad, cpad, caw_ref, vec_ref, ha_buf, hb_buf)
        q2 = q_ref[:, lanes]
        kc2 = kc_ref[:, lanes].astype(BF16)
        vc2 = vc_ref[:, lanes].astype(BF16)
        o2 = None
        for e in range(HEADS_PER_LANE_TILE):
            h = mt * HEADS_PER_LANE_TILE + e
            qm = jnp.where(masks[e], q2, jnp.zeros_like(q2))
            s_ctx = _dot_t(qm, kc2)
            s_loc = jnp.concatenate(
                [_dot_t(qm[q_rows[rr]], k_ref[pl.ds(win_start[rr], n_win), lanes]) + local_bias(h, rr, anchors)
                 for rr in range(ROWS_PER_TILE)], axis=0)
            mx = jnp.maximum(jnp.max(s_loc, axis=-1, keepdims=True), jnp.max(s_ctx, axis=-1, keepdims=True))
            p_l = jnp.exp2(s_loc - mx)
            p_c = jnp.exp2(s_ctx - mx)
            inv = 1.0 / (jnp.sum(p_l, axis=-1, keepdims=True) + jnp.sum(p_c, axis=-1, keepdims=True))
            p_l = p_l.astype(BF16)
            oe = jnp.concatenate(
                [_dot(p_l[q_rows[rr]], v_ref[pl.ds(win_start[rr], n_win), lanes])
                 for rr in range(ROWS_PER_TILE)], axis=0) + _dot(p_c.astype(BF16), vc2)
            oe = oe * inv
            o2 = oe if o2 is None else jnp.where(masks[e], oe, o2)
        o_buf[:, lanes] = o2.astype(BF16)
    ya, yb = _conv_outputs(ha_buf[...], hb_buf[...], bg_ref[...], vec_ref, wa_ref, wb_ref)
    yc = _dot(o_buf[...], wc_ref[...])
    _merge(x_ref, gate_ref, m_ref[5:6, :], ya, yb, yc, wm_ref, o_ref)


def _mixer_weight_specs(layer):
    shapes = [(CONV_A + 1, D_A), (SUBLANES, D_A),
              (D_A, D_MODEL), (D_B, D_MODEL), (D_C, D_MODEL), (D_MODEL, D_MODEL)]
    return [_layer_resident(shape, layer) for shape in shapes]


def _mix_ctx(x, mods, layer, pieces, weights, seq):
    n = x.shape[0]
    a, bg, ch, q, k, v, gates = pieces
    tok = lambda width: pl.BlockSpec((seq, width), lambda i: (i, 0))
    return pl.pallas_call(
        _mix_ctx_kernel,
        grid=(n // seq,),
        in_specs=[tok(D_MODEL), _mods_spec(layer, lambda i: 0),
                  tok(D_A), tok(D_B), tok(D_B), tok(D_C), tok(D_C), tok(D_C), tok(3 * D_MODEL)]
                 + _mixer_weight_specs(layer),
        out_specs=tok(D_MODEL),
        out_shape=jax.ShapeDtypeStruct((n, D_MODEL), F32),
        scratch_shapes=_mixer_scratch(seq),
        compiler_params=_params(1),
        name="mix_ctx",
    )(x, mods, a, bg, ch, q, k, v, gates, *weights)


def _mix_lat(x, mods, layer, pieces, cache_k, cache_v, bias, weights, batch, seq):
    a, bg, ch, q, k, v, gates = (p.reshape(batch, seq, p.shape[-1]) for p in pieces)
    t = ROWS_PER_TILE * GRID_W
    halo_per_tile = t // HALO
    n_halo = seq // HALO
    past = cache_k.shape[2]
    tok = lambda width: pl.BlockSpec((None, t, width), lambda b, i: (b, i, 0))
    prev = lambda width: pl.BlockSpec((None, HALO, width),
                                      lambda b, i: (b, jnp.maximum(i * halo_per_tile - 1, 0), 0))
    nxt = lambda width: pl.BlockSpec((None, HALO, width),
                                     lambda b, i: (b, jnp.minimum((i + 1) * halo_per_tile, n_halo - 1), 0))
    seq_resident = pl.BlockSpec((None, seq, D_C), lambda b, i: (b, 0, 0), pipeline_mode=pl.Buffered(1))
    cache = pl.BlockSpec((None, None, past, D_C), lambda b, i: (b, layer, 0, 0), pipeline_mode=pl.Buffered(1))
    out = pl.pallas_call(
        _mix_lat_kernel,
        grid=(batch, seq // t),
        in_specs=[tok(D_MODEL), _mods_spec(layer, lambda b, i: 1 + b),
                  tok(D_A), prev(D_A), nxt(D_A), tok(D_B), tok(D_B), prev(D_B), nxt(D_B),
                  tok(D_C), seq_resident, seq_resident, cache, cache,
                  _layer_resident((N_HEADS, ROWS_PER_CHUNK, N_BIAS_CHUNKS, GRID_W, LANES), layer),
                  tok(3 * D_MODEL)]
                 + _mixer_weight_specs(layer),
        out_specs=tok(D_MODEL),
        out_shape=jax.ShapeDtypeStruct((batch, seq, D_MODEL), F32),
        scratch_shapes=_mixer_scratch(t),
        compiler_params=_params(2),
        name="mix_lat",
    )(x.reshape(batch, seq, D_MODEL), mods, a, a, a, bg, ch, ch, ch, q, k, v,
      cache_k, cache_v, bias, gates, *weights)
    return out.reshape(batch * seq, D_MODEL)


def _local_bias_table(rpb):
    qc = np.arange(GRID_W)[:, None]
    kc = np.arange(GRID_W)[None, :]
    win_start = np.clip(qc - WIN_COLS // 2, 0, GRID_W - WIN_COLS)
    valid = (kc >= win_start) & (kc < win_start + WIN_COLS)
    assert np.all(np.abs(kc - qc)[valid] < WIN_COLS)
    pick = (valid[:, :, None] & (np.arange(2 * WIN_COLS - 1) == (kc - qc + WIN_COLS - 1)[:, :, None]))
    masked = np.where(valid, 0.0, NEG_INF).astype(np.float32)
    rel_rows = (np.arange(ROWS_PER_CHUNK)[:, None, None] + ROWS_PER_CHUNK * np.arange(N_BIAS_CHUNKS)[None, :, None]
                + np.arange(ROWS_PER_CHUNK)[None, None, :])
    chunk_rows = rpb[:, :, rel_rows, :] * LOG2E
    tab = jnp.einsum('lhpjdx,ckx->lhpjcdk', chunk_rows, jnp.asarray(pick, F32),
                     precision=lax.Precision.HIGHEST) + masked[:, None, :]
    return tab.reshape(rpb.shape[0], N_HEADS, ROWS_PER_CHUNK, N_BIAS_CHUNKS, GRID_W, LANES)


def kernel(x_prompt, x_sample, cache_k, cache_v, c, c_ctx, w_ada, b_ada, g_ff1, w_ff1_gate, w_ff1_up,
           w_ff1_down, g_mix, w_in, conv_a_w, conv_a_b, ln_a_g, ln_a_b, w_a_out, conv_b_w, w_b_out,
           q_norm_g, k_norm_g, rpb, w_c_out, w_merge, g_ff2, w_ff2_gate, w_ff2_up, w_ff2_down):
    batch, seq, _ = x_prompt.shape
    dec_batch, dec_seq, _ = x_sample.shape
    past = cache_k.shape[2]
    assert dec_batch + 1 <= MOD_ROWS and dec_seq % (ROWS_PER_TILE * GRID_W) == 0
    assert (batch * seq) % TM == 0 and dec_seq % TM == 0 and seq % HALO == 0 and TM % seq == 0

    cvecs = jnp.zeros((MOD_ROWS, D_MODEL), F32).at[0].set(c_ctx).at[1:1 + dec_batch].set(c)
    mods = _modulation(cvecs, w_ada, b_ada)
    ctx_row = lambda i: 0
    lat_row = lambda i: 1 + (i * TM) // dec_seq

    def tile_rows(*vectors):
        packed = jnp.concatenate(vectors, axis=1)
        return jnp.pad(packed, ((0, 0), (0, SUBLANES - packed.shape[1]), (0, 0)))

    row = lambda v: v[:, None, :]
    bf16 = lambda w: w.astype(BF16)
    head = np.arange(D_C) // HEAD_DIM
    bd = jnp.asarray((head[:, None] == head[None, :]) / HEAD_DIM, BF16)
    ff1 = (tile_rows(row(g_ff1)), bf16(w_ff1_gate), bf16(w_ff1_up), bf16(w_ff1_down))
    ff2 = (tile_rows(row(g_ff2)), bf16(w_ff2_gate), bf16(w_ff2_up), bf16(w_ff2_down))
    qg = jnp.tile(q_norm_g, (1, N_HEADS)) * (HEAD_DIM ** -0.5 * LOG2E)
    kg = jnp.tile(k_norm_g, (1, N_HEADS))
    proj_w = (tile_rows(row(g_mix)), bf16(w_in), bd, tile_rows(row(qg), row(kg)))
    vec = tile_rows(row(conv_a_b), row(ln_a_g), row(ln_a_b), conv_b_w)
    mix_w = (jnp.pad(conv_a_w, ((0, 0), (0, 1), (0, 0))), vec,
             bf16(w_a_out), bf16(w_b_out), bf16(w_c_out), bf16(w_merge))
    bias = _local_bias_table(rpb)
    ck = cache_k.reshape(dec_batch, DEPTH, past, D_C)
    cv = cache_v.reshape(dec_batch, DEPTH, past, D_C)

    h_ctx = x_prompt.reshape(batch * seq, D_MODEL)
    h_lat = x_sample.reshape(dec_batch * dec_seq, D_MODEL)
    new_kv = ()
    for l in range(DEPTH):
        h_ctx = _ffn(h_ctx, mods, l, 0, ctx_row, *ff1)
        *pieces, new_k, new_v = _proj(h_ctx, mods, l, ctx_row, *proj_w, new_kv_seq=seq, new_kv=new_kv)
        new_kv = (new_k, new_v)
        h_ctx = _mix_ctx(h_ctx, mods, l, pieces, mix_w, seq)
        h_ctx = _ffn(h_ctx, mods, l, 6, ctx_row, *ff2)

        h_lat = _ffn(h_lat, mods, l, 0, lat_row, *ff1)
        pieces = _proj(h_lat, mods, l, lat_row, *proj_w)
        h_lat = _mix_lat(h_lat, mods, l, pieces, ck, cv, bias, mix_w, dec_batch, dec_seq)
        h_lat = _ffn(h_lat, mods, l, 6, lat_row, *ff2)

    heads = lambda t: t.reshape(batch, DEPTH, seq, N_HEADS, HEAD_DIM)
    return (h_ctx.reshape(batch, seq, D_MODEL), h_lat.reshape(dec_batch, dec_seq, D_MODEL),
            heads(new_kv[0]), heads(new_kv[1]))
```
